```python
import math
import jax
import jax.numpy as jnp
from jax import lax
import numpy as np

D_MODEL = 1024
BATCH = 4
SEQ = 4096
DEPTH = 4
DEC_BATCH = 128
DEC_SEQ = 1
PAST_LEN = 8192
PAGE_SIZE = 128

H_A = 8
HD_A = 64
A_PATTERNS = ((128, 1), (512, 4), (2048, 16))
WIN_A = max(w for w, _ in A_PATTERNS)
H_B = 4
DK_B = 128
DV_B = 128
CONV_B = 4
CHUNK_B = 64
H_C = 8
KVH_C = 2
HD_C = 64
WIN_C = 128
D_FF = 3 * D_MODEL
CONV_FFN = 3
Q_BLOCK = 128
EPS = 1e-6
NEG_INF = -1e30

WA = H_A * HD_A
WB_QK = H_B * DK_B
WB_V = H_B * DV_B
WC_Q = H_C * HD_C
WC_KV = KVH_C * HD_C
CB_CH = 2 * WB_QK + WB_V
SPLITS = (WA, WA, WA, CB_CH, WB_V, H_B, H_B, WC_Q, WC_KV, WC_KV, 3 * D_MODEL)
D_IN = 3 * WA + CB_CH + WB_V + 2 * H_B + WC_Q + 2 * WC_KV + 3 * D_MODEL

kernel_name = 'hybrid_gated_branch_decoder_step'


def rms_norm(x, g):
    xf = x.astype(jnp.float32)
    y = xf * lax.rsqrt(jnp.mean(xf * xf, axis=-1, keepdims=True) + EPS)
    return (y * g.astype(jnp.float32)).astype(x.dtype)


def l2norm(x):
    xf = x.astype(jnp.float32)
    return xf * lax.rsqrt(jnp.sum(xf * xf, axis=-1, keepdims=True) + EPS)


def alibi_slopes():
    n = H_A + H_C
    s = (2.0 ** (-8.0 * (np.arange(n) + 1) / n)).astype(np.float32)
    return s[H_C:], s[:H_C]


def causal_dwconv(x, past, w, bias=None):
    k = w.shape[0]
    t = x.shape[1]
    xc = jnp.concatenate([past.astype(x.dtype), x], axis=1)
    y = sum(xc[:, j:j + t] * w[j] for j in range(k))
    if bias is not None:
        y = y + bias
    return y, xc[:, t:]


def _blocks(t):
    qb = min(Q_BLOCK, t)
    nb = -(-t // qb)
    return qb, nb, qb * nb


def dilated_attention(q, k_all, v_all, slopes):
    bsz, t, h, hd = q.shape
    past = k_all.shape[1] - t
    qb, nb, tp = _blocks(t)
    q = jnp.pad(q, ((0, 0), (0, tp - t), (0, 0), (0, 0)))
    kp = jnp.pad(k_all, ((0, 0), (WIN_A, tp - t), (0, 0), (0, 0)))
    vp = jnp.pad(v_all, ((0, 0), (WIN_A, tp - t), (0, 0), (0, 0)))
    dist = np.concatenate([np.arange(0, w + 1, d) for w, d in A_PATTERNS])
    rel = WIN_A + np.arange(qb)[:, None] - dist[None, :]
    bias = -slopes[:, None, None] * dist.astype(np.float32)[None, None, :]
    scale = hd ** -0.5

    def block(i):
        start = past + i * qb
        kb = lax.dynamic_slice_in_dim(kp, start, WIN_A + qb, axis=1)[:, rel]
        vb = lax.dynamic_slice_in_dim(vp, start, WIN_A + qb, axis=1)[:, rel]
        qi = lax.dynamic_slice_in_dim(q, i * qb, qb, axis=1)
        s = jnp.einsum('bqhd,bqkhd->bhqk', qi, kb, preferred_element_type=jnp.float32) * scale + bias
        valid = (start + np.arange(qb)[:, None] - dist[None, :]) >= 0
        p = jax.nn.softmax(jnp.where(valid, s, NEG_INF), axis=-1)
        return jnp.einsum('bhqk,bqkhd->bqhd', p.astype(vb.dtype), vb)

    o = lax.map(block, jnp.arange(nb))
    return jnp.moveaxis(o, 0, 1).reshape(bsz, tp, h, hd)[:, :t]


def swa_sink_attention(q, k_all, v_all, slopes, sinks):
    bsz, t, h, hd = q.shape
    kvh = k_all.shape[2]
    grp = h // kvh
    past = k_all.shape[1] - t
    qb, nb, tp = _blocks(t)
    q = jnp.pad(q, ((0, 0), (0, tp - t), (0, 0), (0, 0))).reshape(bsz, tp, kvh, grp, hd)
    kp = jnp.pad(k_all, ((0, 0), (WIN_C, tp - t), (0, 0), (0, 0)))
    vp = jnp.pad(v_all, ((0, 0), (WIN_C, tp - t), (0, 0), (0, 0)))
    nk = WIN_C + qb
    qpos = np.arange(qb)[:, None]
    kpos = np.arange(nk)[None, :]
    dist = WIN_C + qpos - kpos
    in_band = (dist >= 0) & (dist < WIN_C)
    bias = -slopes.reshape(kvh, grp)[:, :, None, None] * dist.astype(np.float32)
    sink = sinks.astype(jnp.float32).reshape(kvh, grp)[:, :, None, None]
    scale = hd ** -0.5

    def block(i):
        start = past + i * qb
        kb = lax.dynamic_slice_in_dim(kp, start, nk, axis=1)
        vb = lax.dynamic_slice_in_dim(vp, start, nk, axis=1)
        qi = lax.dynamic_slice_in_dim(q, i * qb, qb, axis=1)
        s = jnp.einsum('bqhgd,bkhd->bhgqk', qi, kb, preferred_element_type=jnp.float32) * scale + bias
        valid = in_band & ((start - WIN_C + kpos) >= 0)
        s = jnp.where(valid, s, NEG_INF)
        s_sink = jnp.broadcast_to(sink, s.shape[:-1] + (1,))
        p = jax.nn.softmax(jnp.concatenate([s, s_sink], axis=-1), axis=-1)[..., :-1]
        return jnp.einsum('bhgqk,bkhd->bqhgd', p.astype(vb.dtype), vb)

    o = lax.map(block, jnp.arange(nb))
    return jnp.moveaxis(o, 0, 1).reshape(bsz, tp, h, hd)[:, :t]


def gated_delta_rule(q, k, v, g, beta, s0):
    bsz, t, h, dk = q.shape
    dv = v.shape[-1]
    c = min(CHUNK_B, t)
    n = -(-t // c)
    pad = n * c - t

    def chunks(a):
        a = jnp.pad(a.astype(jnp.float32), [(0, 0), (0, pad)] + [(0, 0)] * (a.ndim - 2))
        a = a.reshape((bsz, n, c) + a.shape[2:])
        return jnp.moveaxis(a, 1, 0).swapaxes(2, 3)

    qc = chunks(q) * dk ** -0.5
    kc = chunks(k)
    vc = chunks(v)
    bc = chunks(beta)
    gc = jnp.cumsum(chunks(g), axis=-1)
    incl = np.tril(np.ones((c, c), dtype=bool))
    strict = np.tril(np.ones((c, c), dtype=bool), -1)
    decay = jnp.exp(jnp.where(incl, gc[..., :, None] - gc[..., None, :], -jnp.inf))
    kbeta = kc * bc[..., None]
    a_low = jnp.where(strict, jnp.einsum('nbhid,nbhjd->nbhij', kbeta, kc) * decay, 0.0)
    eye = jnp.eye(c, dtype=jnp.float32)
    tmat = lax.linalg.triangular_solve(eye + a_low, jnp.broadcast_to(eye, a_low.shape), left_side=True, lower=True)
    u = jnp.einsum('nbhij,nbhjd->nbhid', tmat, vc * bc[..., None])
    w = jnp.einsum('nbhij,nbhjd->nbhid', tmat, kbeta * jnp.exp(gc)[..., None])
    qk = jnp.where(incl, jnp.einsum('nbhid,nbhjd->nbhij', qc, kc) * decay, 0.0)

    def step(s, xs):
        q_i, k_i, u_i, w_i, qk_i, g_i = xs
        v_new = u_i - jnp.einsum('bhcd,bhde->bhce', w_i, s)
        o = jnp.einsum('bhcd,bhde->bhce', q_i * jnp.exp(g_i)[..., None], s) + jnp.einsum('bhij,bhje->bhie', qk_i, v_new)
        g_last = g_i[..., -1]
        k_dec = k_i * jnp.exp(g_last[..., None] - g_i)[..., None]
        s = s * jnp.exp(g_last)[..., None, None] + jnp.einsum('bhcd,bhce->bhde', k_dec, v_new)
        return s, o

    s_fin, o = lax.scan(step, s0.astype(jnp.float32), (qc, kc, u, w, qk, gc))
    o = o.swapaxes(2, 3).swapaxes(0, 1).reshape(bsz, n * c, h, dv)[:, :t]
    return o, s_fin


def mix_block(h, ka_p, va_p, kc_p, vc_p, convb_p, ssm_p,
              w_in, w_conv_b, a_log, dt_bias, gdn_norm, sinks, w_br_a, w_br_b, w_br_c, w_out):
    bsz, t, _ = h.shape
    slopes_a, slopes_c = alibi_slopes()
    proj = h @ w_in
    qa, ka, va, qkv_b, z_b, a_dec, b_gate, qc, kc, vc, gates = jnp.split(proj, np.cumsum(SPLITS)[:-1].tolist(), axis=-1)
    ka = ka.reshape(bsz, t, H_A, HD_A)
    va = va.reshape(bsz, t, H_A, HD_A)
    o_a = dilated_attention(qa.reshape(bsz, t, H_A, HD_A),
                            jnp.concatenate([ka_p.astype(ka.dtype), ka], axis=1),
                            jnp.concatenate([va_p.astype(va.dtype), va], axis=1), slopes_a).reshape(bsz, t, WA)
    qkv_b, convb_new = causal_dwconv(qkv_b, convb_p, w_conv_b)
    qkv_b = jax.nn.silu(qkv_b)
    q_b, k_b, v_b = jnp.split(qkv_b, [WB_QK, 2 * WB_QK], axis=-1)
    q_b = l2norm(q_b.reshape(bsz, t, H_B, DK_B))
    k_b = l2norm(k_b.reshape(bsz, t, H_B, DK_B))
    v_b = v_b.reshape(bsz, t, H_B, DV_B)
    beta = jax.nn.sigmoid(b_gate.astype(jnp.float32))
    g = -jnp.exp(a_log.astype(jnp.float32)) * jax.nn.softplus(a_dec.astype(jnp.float32) + dt_bias.astype(jnp.float32))
    o_b, ssm_new = gated_delta_rule(q_b, k_b, v_b, g, beta, ssm_p)
    o_b = (rms_norm(o_b, gdn_norm) * jax.nn.silu(z_b.reshape(bsz, t, H_B, DV_B).astype(jnp.float32))).astype(h.dtype)
    o_b = o_b.reshape(bsz, t, WB_V)
    kc = kc.reshape(bsz, t, KVH_C, HD_C)
    vc = vc.reshape(bsz, t, KVH_C, HD_C)
    o_c = swa_sink_attention(qc.reshape(bsz, t, H_C, HD_C),
                             jnp.concatenate([kc_p.astype(kc.dtype), kc], axis=1),
                             jnp.concatenate([vc_p.astype(vc.dtype), vc], axis=1), slopes_c, sinks).reshape(bsz, t, WC_Q)
    g_a, g_b, g_c = jnp.split(jax.nn.sigmoid(gates), 3, axis=-1)
    merged = g_a * (o_a @ w_br_a) + g_b * (o_b @ w_br_b) + g_c * (o_c @ w_br_c)
    ra = min(WIN_A, t)
    rc = min(WIN_C, t)
    new_state = (ka[:, t - ra:], va[:, t - ra:], kc[:, t - rc:], vc[:, t - rc:], convb_new, ssm_new)
    return merged @ w_out, new_state


def conv_ffn(h, past, w_up, w_conv, b_conv, w_down):
    u, new_buf = causal_dwconv(h @ w_up, past, w_conv, b_conv)
    gate, up = jnp.split(u, 2, axis=-1)
    return (jax.nn.gelu(gate) * up) @ w_down, new_buf


def run_trunk(x, cache_a_k, cache_a_v, cache_c_k, cache_c_v, state_b_conv, state_b_ssm, state_ffn_conv,
              norm_mix, w_in, w_conv_b, a_log, dt_bias, gdn_norm, sinks, w_br_a, w_br_b, w_br_c, w_out,
              norm_ffn, w_up, w_conv_ffn, b_conv_ffn, w_down, norm_final):
    new = [[] for _ in range(7)]
    for l in range(DEPTH):
        h = rms_norm(x, norm_mix[l])
        m, st = mix_block(h, cache_a_k[l], cache_a_v[l], cache_c_k[l], cache_c_v[l], state_b_conv[l], state_b_ssm[l],
                          w_in[l], w_conv_b[l], a_log[l], dt_bias[l], gdn_norm[l], sinks[l],
                          w_br_a[l], w_br_b[l], w_br_c[l], w_out[l])
        x = x + m
        h = rms_norm(x, norm_ffn[l])
        f, conv_new = conv_ffn(h, state_ffn_conv[l], w_up[l], w_conv_ffn[l], b_conv_ffn[l], w_down[l])
        x = x + f
        for lst, s in zip(new, st + (conv_new,)):
            lst.append(s)
    return rms_norm(x, norm_final), [jnp.stack(lst) for lst in new]


def setup_inputs(seed: int = 0) -> dict:
    key = jax.random.key(seed)
    ks = jax.random.split(key, 26)

    def nrm(k, shape, scale):
        return scale * jax.random.normal(k, shape, jnp.float32)

    p_a = min(WIN_A, PAST_LEN)
    p_c = min(WIN_C, PAST_LEN)
    dt = jnp.exp(jax.random.uniform(ks[13], (DEPTH, H_B), jnp.float32, minval=math.log(1e-3), maxval=math.log(1e-1)))
    return {
        'x_prompt': nrm(ks[0], (BATCH, SEQ, D_MODEL), 1.0),
        'x_sample': nrm(ks[1], (DEC_BATCH, DEC_SEQ, D_MODEL), 1.0),
        'cache_a_k': nrm(ks[2], (DEPTH, DEC_BATCH, p_a, H_A, HD_A), 1.0),
        'cache_a_v': nrm(ks[3], (DEPTH, DEC_BATCH, p_a, H_A, HD_A), 1.0),
        'cache_c_k': nrm(ks[4], (DEPTH, DEC_BATCH, p_c, KVH_C, HD_C), 1.0),
        'cache_c_v': nrm(ks[5], (DEPTH, DEC_BATCH, p_c, KVH_C, HD_C), 1.0),
        'state_b_conv': nrm(ks[6], (DEPTH, DEC_BATCH, CONV_B - 1, CB_CH), 1.0),
        'state_b_ssm': nrm(ks[7], (DEPTH, DEC_BATCH, H_B, DK_B, DV_B), DK_B ** -0.5),
        'state_ffn_conv': nrm(ks[8], (DEPTH, DEC_BATCH, CONV_FFN - 1, 2 * D_FF), 1.0),
        'norm_mix': 1.0 + nrm(ks[9], (DEPTH, D_MODEL), 0.02),
        'w_in': nrm(ks[10], (DEPTH, D_MODEL, D_IN), D_MODEL ** -0.5),
        'w_conv_b': nrm(ks[11], (DEPTH, CONV_B, CB_CH), CONV_B ** -0.5),
        'a_log': jnp.log(jax.random.uniform(ks[12], (DEPTH, H_B), jnp.float32, minval=1.0, maxval=16.0)),
        'dt_bias': dt + jnp.log(-jnp.expm1(-dt)),
        'gdn_norm': 1.0 + nrm(ks[14], (DEPTH, DV_B), 0.02),
        'sinks': nrm(ks[15], (DEPTH, H_C), 0.5),
        'w_br_a': nrm(ks[16], (DEPTH, WA, D_MODEL), WA ** -0.5),
        'w_br_b': nrm(ks[17], (DEPTH, WB_V, D_MODEL), WB_V ** -0.5),
        'w_br_c': nrm(ks[18], (DEPTH, WC_Q, D_MODEL), WC_Q ** -0.5),
        'w_out': nrm(ks[19], (DEPTH, D_MODEL, D_MODEL), D_MODEL ** -0.5),
        'norm_ffn': 1.0 + nrm(ks[20], (DEPTH, D_MODEL), 0.02),
        'w_up': nrm(ks[21], (DEPTH, D_MODEL, 2 * D_FF), D_MODEL ** -0.5),
        'w_conv_ffn': nrm(ks[22], (DEPTH, CONV_FFN, 2 * D_FF), CONV_FFN ** -0.5),
        'b_conv_ffn': nrm(ks[23], (DEPTH, 2 * D_FF), 0.02),
        'w_down': nrm(ks[24], (DEPTH, D_FF, D_MODEL), D_FF ** -0.5),
        'norm_final': 1.0 + nrm(ks[25], (D_MODEL,), 0.02),
    }


def reference(x_prompt, x_sample, cache_a_k, cache_a_v, cache_c_k, cache_c_v, state_b_conv, state_b_ssm, state_ffn_conv,
              norm_mix, w_in, w_conv_b, a_log, dt_bias, gdn_norm, sinks, w_br_a, w_br_b, w_br_c, w_out,
              norm_ffn, w_up, w_conv_ffn, b_conv_ffn, w_down, norm_final):
    weights = (norm_mix, w_in, w_conv_b, a_log, dt_bias, gdn_norm, sinks, w_br_a, w_br_b, w_br_c, w_out,
               norm_ffn, w_up, w_conv_ffn, b_conv_ffn, w_down, norm_final)
    bp = x_prompt.shape[0]
    dt = x_prompt.dtype
    y_prompt, st_p = run_trunk(
        x_prompt,
        jnp.zeros((DEPTH, bp, 0, H_A, HD_A), dt), jnp.zeros((DEPTH, bp, 0, H_A, HD_A), dt),
        jnp.zeros((DEPTH, bp, 0, KVH_C, HD_C), dt), jnp.zeros((DEPTH, bp, 0, KVH_C, HD_C), dt),
        jnp.zeros((DEPTH, bp, CONV_B - 1, CB_CH), dt), jnp.zeros((DEPTH, bp, H_B, DK_B, DV_B), jnp.float32),
        jnp.zeros((DEPTH, bp, CONV_FFN - 1, 2 * D_FF), dt), *weights)
    a_k_p, a_v_p, c_k_p, c_v_p, b_conv_p, b_ssm_p, ffn_conv_p = st_p
    y_sample, st_s = run_trunk(x_sample, cache_a_k, cache_a_v, cache_c_k, cache_c_v, state_b_conv, state_b_ssm,
                               state_ffn_conv, *weights)
    a_k_s, a_v_s, c_k_s, c_v_s, b_conv_s, b_ssm_s, ffn_conv_s = st_s
    return (y_prompt, y_sample, a_k_p, a_v_p, c_k_p, c_v_p, b_conv_p, b_ssm_p, ffn_conv_p,
            a_k_s, a_v_s, c_k_s, c_v_s, b_conv_s, b_ssm_s, ffn_conv_s)
```

```python
import functools
import math

import numpy as np
import jax
import jax.numpy as jnp
from jax import lax
from jax.experimental import pallas as pl
from jax.experimental.pallas import tpu as pltpu

D_MODEL = 1024
H_A, HD_A = 8, 64
A_PATTERNS = ((128, 1), (512, 4), (2048, 16))
WIN_A = 2048
H_B, DK_B, DV_B = 4, 128, 128
CONV_B = 4
CHUNK_B = 64
H_C, KVH_C, HD_C = 8, 2, 64
WIN_C = 128
D_FF = 3 * D_MODEL
CONV_FFN = 3
EPS = 1e-6
NEG = -1e30

WA = H_A * HD_A
WB_QK = H_B * DK_B
WB_V = H_B * DV_B
WC_Q = H_C * HD_C
WC_KV = KVH_C * HD_C
CB_CH = 2 * WB_QK + WB_V

LANE = 128
C_HEAD_ORDER = (0, 4, 1, 5, 2, 6, 3, 7)

COL_GATES = 0
COL_QA, COL_KA, COL_VA = 24, 28, 32
COL_CB = 36
COL_ZB = 48
COL_QC = 52
COL_AB = 56
COL_KC, COL_VC = 57, 58
N_COLS = 60 * LANE

VMEM_LIMIT = 48 * 1024 * 1024

f32 = jnp.float32
bf16 = jnp.bfloat16


def _cparams(n_axes):
    return pltpu.CompilerParams(dimension_semantics=("arbitrary",) * n_axes, vmem_limit_bytes=VMEM_LIMIT)


def _alibi_slopes():
    n = H_A + H_C
    s = (2.0 ** (-8.0 * (np.arange(n) + 1) / n)).astype(np.float32)
    return s[H_C:], s[:H_C]


def _sigmoid(x):
    return 1.0 / (1.0 + jnp.exp(-x))


def _silu(x):
    return x * _sigmoid(x)


def _softplus(x):
    return jnp.maximum(x, 0.0) + jnp.log1p(jnp.exp(-jnp.abs(x)))


def _gelu_tanh(x):
    c = math.sqrt(2.0 / math.pi)
    return 0.5 * x * (1.0 + jnp.tanh(c * (x + 0.044715 * (x * x * x))))


def _dot(a, b):
    return jnp.dot(a, b, preferred_element_type=f32)


def _dot_nt(a, b):
    return lax.dot_general(a, b, (((1,), (1,)), ((), ())), preferred_element_type=f32)


def _dot_tn(a, b):
    return lax.dot_general(a, b, (((0,), (0,)), ((), ())), preferred_element_type=f32)


def _dot_f32(a, b):
    return jnp.dot(a, b, preferred_element_type=f32, precision=lax.Precision.HIGHEST)


def _norm_matmul_kernel(x_ref, g_ref, w_ref, o_ref, h_ref):
    @pl.when(pl.program_id(1) == 0)
    def _():
        x = x_ref[...]
        ms = jnp.mean(x * x, axis=-1, keepdims=True)
        h_ref[...] = (x * lax.rsqrt(ms + EPS) * g_ref[...]).astype(bf16)

    o_ref[...] = _dot(h_ref[...], w_ref[...])


def _norm_matmul(x, g, w, tm, tn):
    m, d = x.shape
    n = w.shape[1]
    return pl.pallas_call(
        _norm_matmul_kernel,
        grid=(m // tm, n // tn),
        in_specs=[pl.BlockSpec((tm, d), lambda i, j: (i, 0)),
                  pl.BlockSpec((1, d), lambda i, j: (0, 0)),
                  pl.BlockSpec((d, tn), lambda i, j: (0, j))],
        out_specs=pl.BlockSpec((tm, tn), lambda i, j: (i, j)),
        out_shape=jax.ShapeDtypeStruct((m, n), f32),
        scratch_shapes=[pltpu.VMEM((tm, d), bf16)],
        compiler_params=_cparams(2),
        name="norm_in_proj",
    )(x, g, w)


def _band_bias(mode, dist, slope):
    distf = dist.astype(f32)
    if mode == "dilated":
        mult = jnp.zeros(dist.shape, f32)
        for w, d in A_PATTERNS:
            hit = (dist <= w) & ((dist & (d - 1)) == 0)
            mult = mult + jnp.where(hit, 1.0, 0.0)
        logm = jnp.where(mult > 2.5, math.log(3.0), jnp.where(mult > 1.5, math.log(2.0), 0.0))
        return jnp.where((dist >= 0) & (mult > 0.5), logm - slope * distf, NEG)
    return jnp.where((dist >= 0) & (dist < WIN_C), -slope * distf, NEG)


def _band_attn_kernel(slope_ref, sink_ref, q_ref, k_ref, v_ref, o_ref, bias_ref, *,
                      tq, nd, head_a, head_b, mode, use_sink, scale):
    hp = pl.program_id(0)
    b = pl.program_id(1)
    qi = pl.program_id(2)

    @pl.when((b == 0) & (qi == 0))
    def _():
        row = lax.broadcasted_iota(jnp.int32, (tq, tq), 0)
        col = lax.broadcasted_iota(jnp.int32, (tq, tq), 1)
        for hh in range(2):
            slope = slope_ref[hp * head_a + hh * head_b]
            for d in range(nd):
                bias_ref[hh, d] = _band_bias(mode, d * tq + row - col, slope)

    q = q_ref[...] * scale
    lane = lax.broadcasted_iota(jnp.int32, (tq, LANE), 1)
    nsteps = jnp.minimum(qi, nd - 1) + 1
    outs = []
    for hh in range(2):
        half = (lane >= LANE // 2) if hh else (lane < LANE // 2)
        qh = jnp.where(half, q, 0.0).astype(bf16)

        def body(d, carry, qh=qh, hh=hh):
            m, l, acc = carry
            start = pl.multiple_of((qi - d) * tq, tq)
            kc = k_ref[pl.ds(start, tq), :].astype(bf16)
            vc = v_ref[pl.ds(start, tq), :].astype(bf16)
            s = _dot_nt(qh, kc) + bias_ref[hh, d]
            m_new = jnp.maximum(m, jnp.max(s, axis=-1, keepdims=True))
            p = jnp.exp(s - m_new)
            alpha = jnp.exp(m - m_new)
            l = alpha * l + jnp.sum(p, axis=-1, keepdims=True)
            acc = alpha * acc + _dot(p.astype(bf16), vc)
            return m_new, l, acc

        init = (jnp.full((tq, 1), NEG, f32), jnp.zeros((tq, 1), f32), jnp.zeros((tq, LANE), f32))
        m, l, acc = lax.fori_loop(0, nsteps, body, init)
        if use_sink:
            sink = sink_ref[hp * head_a + hh * head_b]
            m_f = jnp.maximum(m, sink)
            a = jnp.exp(m - m_f)
            l = l * a + jnp.exp(sink - m_f)
            acc = acc * a
        outs.append(acc / l)
    o_ref[...] = jnp.where(lane < LANE // 2, outs[0], outs[1])


def _band_attn(proj, slopes, sinks, *, mode, tq):
    bsz, t, _ = proj.shape
    if mode == "dilated":
        nd = min(WIN_A // tq + 1, t // tq)
        head_a, head_b = 2, 1
        qcol, kcol, vcol = COL_QA, COL_KA, COL_VA
        k_map = lambda hp, b, qi: (b, 0, kcol + hp)
        v_map = lambda hp, b, qi: (b, 0, vcol + hp)
        use_sink, hd = False, HD_A
    else:
        nd = min(-(-WIN_C // tq) + 1, t // tq)
        head_a, head_b = 1, 4
        qcol, kcol, vcol = COL_QC, COL_KC, COL_VC
        k_map = lambda hp, b, qi: (b, 0, kcol)
        v_map = lambda hp, b, qi: (b, 0, vcol)
        use_sink, hd = True, HD_C
    kern = functools.partial(_band_attn_kernel, tq=tq, nd=nd, head_a=head_a, head_b=head_b, mode=mode,
                             use_sink=use_sink, scale=hd ** -0.5)
    smem = pl.BlockSpec(memory_space=pltpu.SMEM)
    return pl.pallas_call(
        kern,
        grid=(4, bsz, t // tq),
        in_specs=[smem, smem,
                  pl.BlockSpec((None, tq, LANE), lambda hp, b, qi: (b, qi, qcol + hp)),
                  pl.BlockSpec((None, t, LANE), k_map),
                  pl.BlockSpec((None, t, LANE), v_map)],
        out_specs=pl.BlockSpec((None, tq, LANE), lambda hp, b, qi: (b, qi, hp)),
        out_shape=jax.ShapeDtypeStruct((bsz, t, 4 * LANE), f32),
        scratch_shapes=[pltpu.VMEM((2, nd, tq, tq), f32)],
        compiler_params=_cparams(3),
        name="band_attn_" + mode,
    )(slopes, sinks, proj, proj, proj)


def _attn_a_sample_kernel(bias_ref, q_ref, kn_ref, vn_ref, k16_ref, k4_ref, k1_ref, v16_ref, v4_ref, v1_ref,
                          o_ref, *, bs, scale):
    sets = ((k16_ref, v16_ref), (k4_ref, v4_ref), (k1_ref, v1_ref))
    for b in range(bs):
        q = q_ref[b] * scale
        s_new = jnp.sum(q * kn_ref[b], axis=-1, keepdims=True)
        scores = []
        m = jnp.broadcast_to(s_new, (H_A, HD_A))[None]
        for si, (k_ref, _) in enumerate(sets):
            s = jnp.sum(k_ref[b] * q[None], axis=-1, keepdims=True) + bias_ref[si]
            scores.append(s)
            m = jnp.maximum(m, jnp.max(s, axis=0, keepdims=True))
        m = m[0]
        p_new = float(len(A_PATTERNS)) * jnp.exp(s_new - m)
        l = p_new
        acc = p_new * vn_ref[b]
        for s, (_, v_ref) in zip(scores, sets):
            p = jnp.exp(s - m[None])
            l = l + jnp.sum(p, axis=0)
            acc = acc + jnp.sum(p * v_ref[b], axis=0)
        o_ref[b] = acc / l


def _attn_a_sample(layer, q, kn, vn, cache_k, cache_v, slopes, bs):
    _, s, p, h, hd = cache_k.shape
    rows = 128
    assert p == WIN_A and all(wd // dl == rows for wd, dl in A_PATTERNS)
    dist = np.stack([dl * (rows - np.arange(rows)) for _, dl in A_PATTERNS[::-1]]).astype(np.float32)
    bias = np.broadcast_to(-dist[:, :, None, None] * np.asarray(slopes, np.float32)[None, None, :, None],
                           (len(A_PATTERNS), rows, h, hd))
    views, specs = [], []
    for c in (cache_k, cache_v):
        views += [c.reshape(-1, s, p // 16, 16, h, hd), c.reshape(-1, s, p // 4, 4, h, hd), c]
        specs += [pl.BlockSpec((None, bs, rows, None, h, hd), lambda i: (layer, i, 0, 0, 0, 0)),
                  pl.BlockSpec((None, bs, rows, None, h, hd), lambda i: (layer, i, p // 4 // rows - 1, 0, 0, 0)),
                  pl.BlockSpec((None, bs, rows, h, hd), lambda i: (layer, i, p // rows - 1, 0, 0))]
    tok = pl.BlockSpec((bs, h, hd), lambda i: (i, 0, 0))
    return pl.pallas_call(
        functools.partial(_attn_a_sample_kernel, bs=bs, scale=hd ** -0.5),
        grid=(s // bs,),
        in_specs=[pl.BlockSpec(bias.shape, lambda i: (0, 0, 0, 0)), tok, tok, tok] + specs,
        out_specs=tok,
        out_shape=jax.ShapeDtypeStruct((s, h, hd), f32),
        compiler_params=_cparams(1),
        name="attn_a_sample",
    )(jnp.asarray(bias), q, kn, vn, *views)


def _attn_c_sample_kernel(slope_ref, sink_ref, q_ref, kn_ref, vn_ref, k_ref, v_ref, o_ref, *, bs, scale):
    rows = k_ref.shape[1]
    prow = lax.broadcasted_iota(jnp.int32, (H_C, LANE), 0)
    plane = lax.broadcasted_iota(jnp.int32, (H_C, LANE), 1)
    halfmask = (plane // HD_C) == (prow % 2)
    j = lax.broadcasted_iota(jnp.int32, (H_C, rows), 1)
    dist = rows - j
    slope = slope_ref[...]
    sink = sink_ref[...]
    bias = jnp.where(dist < WIN_C, -slope * dist.astype(f32), NEG)
    for b in range(bs):
        tiles = []
        for t in range(WC_Q // LANE):
            qt = q_ref[b:b + 1, t * LANE:(t + 1) * LANE] * scale
            tiles += [qt, qt]
        qblk = jnp.where(halfmask, jnp.concatenate(tiles, axis=0), 0.0)
        kn = kn_ref[b:b + 1, :]
        vn = vn_ref[b:b + 1, :]
        s_new = jnp.sum(qblk * kn, axis=-1, keepdims=True)
        s = _dot_nt(qblk.astype(bf16), k_ref[b].astype(bf16)) + bias
        m = jnp.maximum(jnp.maximum(s_new, jnp.max(s, axis=-1, keepdims=True)), sink)
        p = jnp.exp(s - m)
        p_new = jnp.exp(s_new - m)
        l = jnp.sum(p, axis=-1, keepdims=True) + p_new + jnp.exp(sink - m)
        acc = _dot(p.astype(bf16), v_ref[b].astype(bf16)) + p_new * vn
        o = jnp.where(halfmask, acc / l, 0.0)
        for t in range(WC_Q // LANE):
            o_ref[b:b + 1, t * LANE:(t + 1) * LANE] = o[2 * t:2 * t + 1, :] + o[2 * t + 1:2 * t + 2, :]


def _attn_c_sample(proj_s, cache_k, cache_v, slopes, sinks, bs):
    s, p, w = cache_k.shape
    assert p == WIN_C and w == LANE
    col = pl.BlockSpec((H_C, 1), lambda i: (0, 0))
    cache = pl.BlockSpec((bs, p, w), lambda i: (i, 0, 0))
    return pl.pallas_call(
        functools.partial(_attn_c_sample_kernel, bs=bs, scale=HD_C ** -0.5),
        grid=(s // bs,),
        in_specs=[col, col,
                  pl.BlockSpec((bs, WC_Q), lambda i: (i, COL_QC * LANE // WC_Q)),
                  pl.BlockSpec((bs, LANE), lambda i: (i, COL_KC)),
                  pl.BlockSpec((bs, LANE), lambda i: (i, COL_VC)),
                  cache, cache],
        out_specs=pl.BlockSpec((bs, WC_Q), lambda i: (i, 0)),
        out_shape=jax.ShapeDtypeStruct((s, WC_Q), f32),
        compiler_params=_cparams(1),
        name="attn_c_sample",
    )(slopes, sinks, proj_s, proj_s, proj_s, cache_k, cache_v)


def _gdn_gates(ab, gp_ref):
    g = -jnp.exp(gp_ref[0:1, :]) * _softplus(ab + gp_ref[1:2, :])
    return g, _sigmoid(ab)


def _l2norm(x):
    return x * lax.rsqrt(jnp.sum(x * x, axis=-1, keepdims=True) + EPS)


def _gated_out(o, z, gn):
    y = o * lax.rsqrt(jnp.mean(o * o, axis=-1, keepdims=True) + EPS) * gn
    return y * _silu(z)


def _gdn_prompt_kernel(x_ref, z_ref, ab_ref, wconv_ref, gp_ref, gn_ref, o_ref, s_ref, buf_ref, act_ref, *, tb):
    ti = pl.program_id(1)
    c = CHUNK_B

    @pl.when(ti == 0)
    def _():
        buf_ref[0:8, :] = jnp.zeros((8, CB_CH), f32)
        s_ref[...] = jnp.zeros(s_ref.shape, f32)

    buf_ref[8:8 + tb, :] = x_ref[...]
    for cb in range(CB_CH // LANE):
        sl = slice(cb * LANE, (cb + 1) * LANE)
        y = buf_ref[pl.ds(8 - (CONV_B - 1), tb), sl] * wconv_ref[0:1, sl]
        for jw in range(1, CONV_B):
            y = y + buf_ref[pl.ds(8 - (CONV_B - 1) + jw, tb), sl] * wconv_ref[jw:jw + 1, sl]
        act_ref[:, sl] = _silu(y)
    buf_ref[0:8, :] = buf_ref[tb:tb + 8, :]

    g, beta = _gdn_gates(ab_ref[...], gp_ref)
    rowc = lax.broadcasted_iota(jnp.int32, (tb, LANE), 0) & (c - 1)
    sh = 1
    while sh < c:
        g = g + jnp.where(rowc >= sh, pltpu.roll(g, sh, axis=0), 0.0)
        sh *= 2
    g_t = jnp.transpose(g)

    ri = lax.broadcasted_iota(jnp.int32, (c, c), 0)
    ci = lax.broadcasted_iota(jnp.int32, (c, c), 1)
    incl = ci <= ri
    strict = ci < ri
    eye = jnp.where(ci == ri, 1.0, 0.0)
    gn = gn_ref[...]

    for ch in range(tb // c):
        r0 = ch * c
        for h in range(H_B):
            gcol = g[r0:r0 + c, h:h + 1]
            grow = g_t[h:h + 1, r0:r0 + c]
            bcol = beta[r0:r0 + c, H_B + h:H_B + h + 1]
            q = _l2norm(act_ref[r0:r0 + c, h * DK_B:(h + 1) * DK_B]) * (DK_B ** -0.5)
            k = _l2norm(act_ref[r0:r0 + c, WB_QK + h * DK_B:WB_QK + (h + 1) * DK_B])
            v = act_ref[r0:r0 + c, 2 * WB_QK + h * DV_B:2 * WB_QK + (h + 1) * DV_B]
            decay = jnp.exp(jnp.where(incl, gcol - grow, NEG))
            eg = jnp.exp(gcol)
            kb = k * bcol
            kbf = k.astype(bf16)
            a_low = jnp.where(strict, _dot_nt(kb.astype(bf16), kbf) * decay, 0.0)
            pw = -a_low
            tmat = eye + pw
            for _ in range(int(math.log2(c)) - 1):
                pw = _dot_f32(pw, pw)
                tmat = tmat + _dot_f32(tmat, pw)
            tb16 = tmat.astype(bf16)
            u = _dot(tb16, (v * bcol).astype(bf16))
            w = _dot(tb16, (kb * eg).astype(bf16))
            qk = jnp.where(incl, _dot_nt(q.astype(bf16), kbf) * decay, 0.0)
            s = s_ref[h]
            sb = s.astype(bf16)
            v_new = u - _dot(w.astype(bf16), sb)
            vb = v_new.astype(bf16)
            o = _dot((q * eg).astype(bf16), sb) + _dot(qk.astype(bf16), vb)
            glast = gcol[c - 1:c, :]
            k_dec = k * jnp.exp(glast - gcol)
            s_ref[h] = s * jnp.exp(glast) + _dot_tn(k_dec.astype(bf16), vb)
            zs = z_ref[r0:r0 + c, h * DV_B:(h + 1) * DV_B]
            o_ref[r0:r0 + c, h * DV_B:(h + 1) * DV_B] = _gated_out(o, zs, gn)


def _gdn_prompt(proj, w_conv, gate_params, gdn_norm, tb):
    bsz, t, _ = proj.shape
    const = lambda shape: pl.BlockSpec(shape, lambda b, ti: (0,) * len(shape))
    return pl.pallas_call(
        functools.partial(_gdn_prompt_kernel, tb=tb),
        grid=(bsz, t // tb),
        in_specs=[pl.BlockSpec((None, tb, CB_CH), lambda b, ti: (b, ti, COL_CB * LANE // CB_CH)),
                  pl.BlockSpec((None, tb, WB_V), lambda b, ti: (b, ti, COL_ZB * LANE // WB_V)),
                  pl.BlockSpec((None, tb, LANE), lambda b, ti: (b, ti, COL_AB)),
                  const((CONV_B, CB_CH)), const((8, LANE)), const((1, DV_B))],
        out_specs=[pl.BlockSpec((None, tb, WB_V), lambda b, ti: (b, ti, 0)),
                   pl.BlockSpec((None, H_B, DK_B, DV_B), lambda b, ti: (b, 0, 0, 0))],
        out_shape=[jax.ShapeDtypeStruct((bsz, t, WB_V), f32),
                   jax.ShapeDtypeStruct((bsz, H_B, DK_B, DV_B), f32)],
        scratch_shapes=[pltpu.VMEM((tb + 8, CB_CH), f32), pltpu.VMEM((tb, CB_CH), f32)],
        compiler_params=_cparams(2),
        name="gdn_prompt",
    )(proj, proj, proj, w_conv, gate_params, gdn_norm)


def _gdn_sample_kernel(x_ref, c0_ref, c1_ref, c2_ref, z_ref, ab_ref, s_ref, wconv_ref, gp_ref, gn_ref,
                       o_ref, so_ref, *, bs):
    y = (c0_ref[...] * wconv_ref[0:1, :] + c1_ref[...] * wconv_ref[1:2, :] + c2_ref[...] * wconv_ref[2:3, :]
         + x_ref[...] * wconv_ref[3:4, :])
    act = _silu(y)
    g, beta = _gdn_gates(ab_ref[...], gp_ref)
    eg = jnp.exp(g)
    gn = gn_ref[...]
    rowi = lax.broadcasted_iota(jnp.int32, (bs, DK_B), 0)
    for h in range(H_B):
        q = _l2norm(act[:, h * DK_B:(h + 1) * DK_B]) * (DK_B ** -0.5)
        k = _l2norm(act[:, WB_QK + h * DK_B:WB_QK + (h + 1) * DK_B])
        v = act[:, 2 * WB_QK + h * DV_B:2 * WB_QK + (h + 1) * DV_B]
        egc = eg[:, h:h + 1]
        bc = beta[:, H_B + h:H_B + h + 1]
        qg = q * egc
        ks_rows, qs_rows = [], []
        for b in range(bs):
            lhs = jnp.where(rowi == 0, jnp.broadcast_to(k[b:b + 1], (bs, DK_B)),
                            jnp.where(rowi == 1, jnp.broadcast_to(qg[b:b + 1], (bs, DK_B)), 0.0))
            r = _dot(lhs.astype(bf16), s_ref[b, h].astype(bf16))
            ks_rows.append(r[0:1])
            qs_rows.append(r[1:2])
        ks = jnp.concatenate(ks_rows, axis=0)
        qs = jnp.concatenate(qs_rows, axis=0)
        v_new = bc * v - (bc * egc) * ks
        qk = jnp.sum(q * k, axis=-1, keepdims=True)
        o = qs + qk * v_new
        o_ref[:, h * DV_B:(h + 1) * DV_B] = _gated_out(o, z_ref[:, h * DV_B:(h + 1) * DV_B], gn)
        vb = v_new.astype(bf16)
        for b in range(bs):
            km = jnp.where(rowi == b, k, 0.0).astype(bf16)
            so_ref[b, h] = s_ref[b, h] * egc[b:b + 1, :] + _dot_tn(km, vb)


def _gdn_sample(layer, proj_s, conv_state, ssm_state, w_conv, gate_params, gdn_norm, bs):
    s = proj_s.shape[0]
    const = lambda shape: pl.BlockSpec(shape, lambda i: (0,) * len(shape))
    cst = lambda j: pl.BlockSpec((bs, CB_CH), lambda i: (i, j))
    st = pl.BlockSpec((bs, H_B, DK_B, DV_B), lambda i: (i, 0, 0, 0))
    st_in = pl.BlockSpec((None, bs, H_B, DK_B, DV_B), lambda i: (layer, i, 0, 0, 0))
    return pl.pallas_call(
        functools.partial(_gdn_sample_kernel, bs=bs),
        grid=(s // bs,),
        in_specs=[pl.BlockSpec((bs, CB_CH), lambda i: (i, COL_CB * LANE // CB_CH)),
                  cst(0), cst(1), cst(2),
                  pl.BlockSpec((bs, WB_V), lambda i: (i, COL_ZB * LANE // WB_V)),
                  pl.BlockSpec((bs, LANE), lambda i: (i, COL_AB)),
                  st_in, const((CONV_B, CB_CH)), const((8, LANE)), const((1, DV_B))],
        out_specs=[pl.BlockSpec((bs, WB_V), lambda i: (i, 0)), st],
        out_shape=[jax.ShapeDtypeStruct((s, WB_V), f32),
                   jax.ShapeDtypeStruct(ssm_state.shape[1:], f32)],
        compiler_params=_cparams(1),
        name="gdn_sample",
    )(proj_s, conv_state, conv_state, conv_state, proj_s, proj_s, ssm_state, w_conv, gate_params, gdn_norm)


def _merge_kernel(x_ref, g_ref, oa_ref, ob_ref, oc_ref, wa_ref, wb_ref, wc_ref, wo_ref, y_ref):
    d = D_MODEL
    m = _sigmoid(g_ref[:, 0:d]) * _dot(oa_ref[...].astype(bf16), wa_ref[...])
    m = m + _sigmoid(g_ref[:, d:2 * d]) * _dot(ob_ref[...].astype(bf16), wb_ref[...])
    m = m + _sigmoid(g_ref[:, 2 * d:3 * d]) * _dot(oc_ref[...].astype(bf16), wc_ref[...])
    y_ref[...] = x_ref[...] + _dot(m.astype(bf16), wo_ref[...])


def _merge(x, proj, o_a, o_b, o_c, wa, wb, wc, wo, tm):
    m, d = x.shape
    row = lambda w: pl.BlockSpec((tm, w), lambda i: (i, 0))
    const = lambda a: pl.BlockSpec(a.shape, lambda i: (0, 0))
    return pl.pallas_call(
        _merge_kernel,
        grid=(m // tm,),
        in_specs=[row(d), row(3 * d), row(WA), row(WB_V), row(WC_Q), const(wa), const(wb), const(wc), const(wo)],
        out_specs=row(d),
        out_shape=jax.ShapeDtypeStruct((m, d), f32),
        compiler_params=_cparams(1),
        name="merge_out_proj",
    )(x, proj, o_a, o_b, o_c, wa, wb, wc, wo)


def _ffn_prompt_kernel(x_ref, g_ref, wg_ref, wu_ref, cg_ref, cu_ref, bg_ref, bu_ref, wd_ref,
                       y_ref, sg_ref, su_ref, h_ref, bufg_ref, bufu_ref, carg_ref, caru_ref, *, tm):
    ti = pl.program_id(1)
    j = pl.program_id(2)
    kw = CONV_FFN

    @pl.when(j == 0)
    def _():
        x = x_ref[...]
        ms = jnp.mean(x * x, axis=-1, keepdims=True)
        h_ref[...] = (x * lax.rsqrt(ms + EPS) * g_ref[...]).astype(bf16)
        y_ref[...] = x

    @pl.when(ti == 0)
    def _():
        carg_ref[j] = jnp.zeros(carg_ref.shape[1:], f32)
        caru_ref[j] = jnp.zeros(caru_ref.shape[1:], f32)

    def conv(w_ref, cw_ref, b_ref, buf_ref, car_ref, st_ref):
        buf_ref[0:8, :] = car_ref[j]
        buf_ref[8:8 + tm, :] = _dot(h_ref[...], w_ref[...])
        y = b_ref[...] + buf_ref[pl.ds(8 - (kw - 1), tm), :] * cw_ref[0:1, :]
        for jw in range(1, kw):
            y = y + buf_ref[pl.ds(8 - (kw - 1) + jw, tm), :] * cw_ref[jw:jw + 1, :]
        car_ref[j] = buf_ref[tm:tm + 8, :]
        st_ref[...] = buf_ref[tm + 8 - (kw - 1):tm + 8, :]
        return y

    gate = conv(wg_ref, cg_ref, bg_ref, bufg_ref, carg_ref, sg_ref)
    up = conv(wu_ref, cu_ref, bu_ref, bufu_ref, caru_ref, su_ref)
    act = (_gelu_tanh(gate) * up).astype(bf16)
    y_ref[...] += _dot(act, wd_ref[...])


def _ffn_prompt(x, g, w_up, w_conv, b_conv, w_down, tm, tf):
    bsz, t, d = x.shape
    nj = D_FF // tf
    kw = CONV_FFN
    gcol = lambda shape: pl.BlockSpec(shape, lambda b, ti, j: (0, j))
    ucol = lambda shape: pl.BlockSpec(shape, lambda b, ti, j: (0, nj + j))
    y, sg, su = pl.pallas_call(
        functools.partial(_ffn_prompt_kernel, tm=tm),
        grid=(bsz, t // tm, nj),
        in_specs=[pl.BlockSpec((None, tm, d), lambda b, ti, j: (b, ti, 0)),
                  pl.BlockSpec((1, d), lambda b, ti, j: (0, 0)),
                  gcol((d, tf)), ucol((d, tf)), gcol((kw, tf)), ucol((kw, tf)), gcol((1, tf)), ucol((1, tf)),
                  pl.BlockSpec((tf, d), lambda b, ti, j: (j, 0))],
        out_specs=[pl.BlockSpec((None, tm, d), lambda b, ti, j: (b, ti, 0)),
                   pl.BlockSpec((None, None, kw - 1, tf), lambda b, ti, j: (b, ti, 0, j)),
                   pl.BlockSpec((None, None, kw - 1, tf), lambda b, ti, j: (b, ti, 0, j))],
        out_shape=[jax.ShapeDtypeStruct((bsz, t, d), f32),
                   jax.ShapeDtypeStruct((bsz, t // tm, kw - 1, D_FF), f32),
                   jax.ShapeDtypeStruct((bsz, t // tm, kw - 1, D_FF), f32)],
        scratch_shapes=[pltpu.VMEM((tm, d), bf16),
                        pltpu.VMEM((tm + 8, tf), f32), pltpu.VMEM((tm + 8, tf), f32),
                        pltpu.VMEM((nj, 8, tf), f32), pltpu.VMEM((nj, 8, tf), f32)],
        compiler_params=_cparams(3),
        name="conv_ffn_prompt",
    )(x, g, w_up, w_up, w_conv, w_conv, b_conv, b_conv, w_down)
    return y, jnp.concatenate([sg[:, -1], su[:, -1]], axis=-1)


def _ffn_sample_kernel(x_ref, g_ref, wg_ref, wu_ref, cg_ref, cu_ref, bg_ref, bu_ref, wd_ref,
                       pg0_ref, pg1_ref, pu0_ref, pu1_ref, y_ref, ug_ref, uu_ref, h_ref):
    @pl.when(pl.program_id(0) == 0)
    def _():
        x = x_ref[...]
        ms = jnp.mean(x * x, axis=-1, keepdims=True)
        h_ref[...] = (x * lax.rsqrt(ms + EPS) * g_ref[...]).astype(bf16)
        y_ref[...] = x

    def conv(w_ref, cw_ref, b_ref, p0_ref, p1_ref, u_ref):
        u = _dot(h_ref[...], w_ref[...])
        u_ref[...] = u
        return b_ref[...] + p0_ref[...] * cw_ref[0:1, :] + p1_ref[...] * cw_ref[1:2, :] + u * cw_ref[2:3, :]

    gate = conv(wg_ref, cg_ref, bg_ref, pg0_ref, pg1_ref, ug_ref)
    up = conv(wu_ref, cu_ref, bu_ref, pu0_ref, pu1_ref, uu_ref)
    y_ref[...] += _dot((_gelu_tanh(gate) * up).astype(bf16), wd_ref[...])


def _ffn_sample(x, g, w_up, w_conv, b_conv, w_down, state, tf):
    s, d = x.shape
    nj = D_FF // tf
    kw = CONV_FFN
    assert kw == 3
    col = lambda shape, off: pl.BlockSpec(shape, lambda j: (0, off + j))
    y, ug, uu = pl.pallas_call(
        _ffn_sample_kernel,
        grid=(nj,),
        in_specs=[pl.BlockSpec((s, d), lambda j: (0, 0)), pl.BlockSpec((1, d), lambda j: (0, 0)),
                  col((d, tf), 0), col((d, tf), nj), col((kw, tf), 0), col((kw, tf), nj),
                  col((1, tf), 0), col((1, tf), nj),
                  pl.BlockSpec((tf, d), lambda j: (j, 0)),
                  col((s, tf), 0), col((s, tf), 2 * nj), col((s, tf), nj), col((s, tf), 3 * nj)],
        out_specs=[pl.BlockSpec((s, d), lambda j: (0, 0)), col((s, tf), 0), col((s, tf), 0)],
        out_shape=[jax.ShapeDtypeStruct((s, d), f32),
                   jax.ShapeDtypeStruct((s, D_FF), f32), jax.ShapeDtypeStruct((s, D_FF), f32)],
        scratch_shapes=[pltpu.VMEM((s, d), bf16)],
        compiler_params=_cparams(1),
        name="conv_ffn_sample",
    )(x, g, w_up, w_up, w_conv, w_conv, b_conv, b_conv, w_down, state, state, state, state)
    return y, jnp.concatenate([ug, uu], axis=-1)


def _final_norm_kernel(x_ref, g_ref, y_ref):
    x = x_ref[...]
    ms = jnp.mean(x * x, axis=-1, keepdims=True)
    y_ref[...] = x * lax.rsqrt(ms + EPS) * g_ref[...]


def _final_norm(x, g, tm):
    m, d = x.shape
    return pl.pallas_call(
        _final_norm_kernel,
        grid=(m // tm,),
        in_specs=[pl.BlockSpec((tm, d), lambda i: (i, 0)), pl.BlockSpec((1, d), lambda i: (0, 0))],
        out_specs=pl.BlockSpec((tm, d), lambda i: (i, 0)),
        out_shape=jax.ShapeDtypeStruct((m, d), f32),
        compiler_params=_cparams(1),
        name="final_norm",
    )(x, g)


def _pack_w_in(w_in):
    splits = np.cumsum([WA, WA, WA, CB_CH, WB_V, H_B, H_B, WC_Q, WC_KV, WC_KV])
    qa, ka, va, cb, zb, adec, bgate, qc, kc, vc, gates = jnp.split(w_in, splits.tolist(), axis=-1)
    lead = w_in.shape[:-1]
    qc = qc.reshape(lead + (H_C, HD_C))[..., list(C_HEAD_ORDER), :].reshape(lead + (WC_Q,))
    ab = jnp.concatenate([adec, bgate, jnp.zeros(lead + (LANE - 2 * H_B,), w_in.dtype)], axis=-1)
    pad = jnp.zeros(lead + (LANE,), w_in.dtype)
    packed = jnp.concatenate([gates, qa, ka, va, cb, zb, qc, ab, kc, vc, pad], axis=-1)
    assert packed.shape[-1] == N_COLS
    return packed.astype(bf16)


def _pick(n, pref):
    for t in pref:
        if n % t == 0:
            return t
    return n


def kernel(x_prompt, x_sample, cache_a_k, cache_a_v, cache_c_k, cache_c_v, state_b_conv, state_b_ssm, state_ffn_conv, norm_mix, w_in, w_conv_b, a_log, dt_bias, gdn_norm, sinks, w_br_a, w_br_b, w_br_c, w_out, norm_ffn, w_up, w_conv_ffn, b_conv_ffn, w_down, norm_final):
    depth = w_in.shape[0]
    bp, t, d = x_prompt.shape
    ns = x_sample.shape[0]
    assert x_sample.shape[1] == 1 and d == D_MODEL and t % 256 == 0

    slopes_a, slopes_c = _alibi_slopes()
    order = list(C_HEAD_ORDER)
    slopes_c_col = jnp.asarray(slopes_c[order]).reshape(H_C, 1)

    w_in_p = _pack_w_in(w_in)
    w_br_a16 = w_br_a.astype(bf16)
    w_br_b16 = w_br_b.astype(bf16)
    w_br_c16 = w_br_c.reshape(depth, H_C, HD_C, d)[:, order].reshape(depth, WC_Q, d).astype(bf16)
    w_out16 = w_out.astype(bf16)
    w_up16 = w_up.astype(bf16)
    w_down16 = w_down.astype(bf16)
    gate_params = jnp.zeros((depth, 8, LANE), f32)
    gate_params = gate_params.at[:, 0, :H_B].set(a_log.astype(f32)).at[:, 1, :H_B].set(dt_bias.astype(f32))

    tm_p = _pick(bp * t, (1024, 512, 256))
    tm_ffn = _pick(t, (512, 256))
    tn = _pick(N_COLS, (512,))
    tf = 512
    bs = _pick(ns, (8,))
    bs_a = _pick(ns, (4,))

    xp = x_prompt.reshape(bp * t, d)
    xs = x_sample.reshape(ns, d)
    outs = {k: [] for k in ("akp", "avp", "ckp", "cvp", "bcp", "bsp", "fcp", "aks", "avs", "cks", "cvs", "bcs",
                            "bss", "fcs")}
    ra = min(WIN_A, t)
    rc = min(WIN_C, t)
    for l in range(depth):
        g_mix = norm_mix[l].reshape(1, d)
        sinks_l = sinks[l].astype(f32)
        proj = _norm_matmul(xp, g_mix, w_in_p[l], tm_p, tn)
        proj3 = proj.reshape(bp, t, N_COLS)
        o_a = _band_attn(proj3, jnp.asarray(slopes_a), sinks_l, mode="dilated", tq=256)
        o_c = _band_attn(proj3, jnp.asarray(slopes_c), sinks_l, mode="window", tq=128)
        o_b, ssm_p = _gdn_prompt(proj3, w_conv_b[l], gate_params[l], gdn_norm[l].reshape(1, DV_B), tb=128)
        xp = _merge(xp, proj, o_a.reshape(bp * t, WA), o_b.reshape(bp * t, WB_V), o_c.reshape(bp * t, WC_Q),
                    w_br_a16[l], w_br_b16[l], w_br_c16[l], w_out16[l], 256)
        y3, fc_p = _ffn_prompt(xp.reshape(bp, t, d), norm_ffn[l].reshape(1, d), w_up16[l], w_conv_ffn[l],
                               b_conv_ffn[l].reshape(1, 2 * D_FF), w_down16[l], tm_ffn, tf)
        xp = y3.reshape(bp * t, d)
        ka = proj3[:, t - ra:, COL_KA * LANE:COL_KA * LANE + WA]
        va = proj3[:, t - ra:, COL_VA * LANE:COL_VA * LANE + WA]
        outs["akp"].append(ka.reshape(bp, ra, H_A, HD_A))
        outs["avp"].append(va.reshape(bp, ra, H_A, HD_A))
        outs["ckp"].append(proj3[:, t - rc:, COL_KC * LANE:(COL_KC + 1) * LANE].reshape(bp, rc, KVH_C, HD_C))
        outs["cvp"].append(proj3[:, t - rc:, COL_VC * LANE:(COL_VC + 1) * LANE].reshape(bp, rc, KVH_C, HD_C))
        outs["bcp"].append(proj3[:, t - (CONV_B - 1):, COL_CB * LANE:COL_CB * LANE + CB_CH])
        outs["bsp"].append(ssm_p)
        outs["fcp"].append(fc_p)
        proj_s = _norm_matmul(xs, g_mix, w_in_p[l], ns, tn)
        heads_a = lambda col: proj_s[:, col * LANE:col * LANE + WA].reshape(ns, H_A, HD_A)
        ka_s, va_s = heads_a(COL_KA), heads_a(COL_VA)
        o_a = _attn_a_sample(l, heads_a(COL_QA), ka_s, va_s, cache_a_k, cache_a_v, slopes_a, bs_a).reshape(ns, WA)
        pc = cache_c_k.shape[2]
        o_c = _attn_c_sample(proj_s, cache_c_k[l].reshape(ns, pc, WC_KV), cache_c_v[l].reshape(ns, pc, WC_KV),
                             slopes_c_col, sinks_l[jnp.asarray(order)].reshape(H_C, 1), bs)
        conv_st = state_b_conv[l]
        o_b, ssm_s = _gdn_sample(l, proj_s, conv_st.reshape(ns, (CONV_B - 1) * CB_CH), state_b_ssm, w_conv_b[l],
                                 gate_params[l], gdn_norm[l].reshape(1, DV_B), bs)
        xs = _merge(xs, proj_s, o_a, o_b, o_c, w_br_a16[l], w_br_b16[l], w_br_c16[l], w_out16[l], ns)
        ffn_st = state_ffn_conv[l]
        xs, u_s = _ffn_sample(xs, norm_ffn[l].reshape(1, d), w_up16[l], w_conv_ffn[l],
                              b_conv_ffn[l].reshape(1, 2 * D_FF), w_down16[l],
                              ffn_st.reshape(ns, (CONV_FFN - 1) * 2 * D_FF), tf)
        outs["aks"].append(ka_s[:, None])
        outs["avs"].append(va_s[:, None])
        outs["cks"].append(proj_s[:, COL_KC * LANE:(COL_KC + 1) * LANE].reshape(ns, 1, KVH_C, HD_C))
        outs["cvs"].append(proj_s[:, COL_VC * LANE:(COL_VC + 1) * LANE].reshape(ns, 1, KVH_C, HD_C))
        xb_new = proj_s[:, COL_CB * LANE:COL_CB * LANE + CB_CH]
        outs["bcs"].append(jnp.concatenate([conv_st[:, 1:], xb_new[:, None, :]], axis=1))
        outs["bss"].append(ssm_s)
        outs["fcs"].append(jnp.concatenate([ffn_st[:, 1:], u_s[:, None, :]], axis=1))

    g_fin = norm_final.reshape(1, d)
    y_prompt = _final_norm(xp, g_fin, tm_p).reshape(bp, t, d)
    y_sample = _final_norm(xs, g_fin, ns).reshape(ns, 1, d)
    st = {k: jnp.stack(v) for k, v in outs.items()}
    return (y_prompt, y_sample, st["akp"], st["avp"], st["ckp"], st["cvp"], st["bcp"], st["bsp"], st["fcp"],
            st["aks"], st["avs"], st["cks"], st["cvs"], st["bcs"], st["bss"], st["fcs"])
```

```python
import functools
import math

import numpy as np
import jax
import jax.numpy as jnp
from jax import lax
from jax.experimental import pallas as pl
from jax.experimental.pallas import tpu as pltpu

D_MODEL = 1024
H_A, HD_A = 8, 64
A_PATTERNS = ((128, 1), (512, 4), (2048, 16))
WIN_A = 2048
H_B, DK_B, DV_B = 4, 128, 128
CONV_B = 4
CHUNK_B = 64
H_C, KVH_C, HD_C = 8, 2, 64
WIN_C = 128
D_FF = 3 * D_MODEL
CONV_FFN = 3
EPS = 1e-6
NEG = -1e30
LOG2E = math.log2(math.e)

WA = H_A * HD_A
WB_QK = H_B * DK_B
WB_V = H_B * DV_B
WC_Q = H_C * HD_C
WC_KV = KVH_C * HD_C
CB_CH = 2 * WB_QK + WB_V

LANE = 128
C_HEAD_ORDER = (0, 4, 1, 5, 2, 6, 3, 7)

COL_GATES = 0
COL_QA, COL_KA, COL_VA = 24, 28, 32
COL_CB = 36
COL_ZB = 48
COL_QC = 52
COL_AB = 56
COL_KC, COL_VC = 57, 58
N_COLS = 60 * LANE

VMEM_LIMIT = 48 * 1024 * 1024

f32 = jnp.float32
bf16 = jnp.bfloat16


def _cparams(n_axes):
    return pltpu.CompilerParams(dimension_semantics=("arbitrary",) * n_axes, vmem_limit_bytes=VMEM_LIMIT)


def _alibi_slopes():
    n = H_A + H_C
    s = (2.0 ** (-8.0 * (np.arange(n) + 1) / n)).astype(np.float32)
    return s[H_C:], s[:H_C]


def _sigmoid(x):
    return 1.0 / (1.0 + jnp.exp(-x))


def _silu(x):
    return x * _sigmoid(x)


def _softplus(x):
    return jnp.maximum(x, 0.0) + jnp.log1p(jnp.exp(-jnp.abs(x)))


def _gelu_tanh(x):
    c = math.sqrt(2.0 / math.pi)
    return 0.5 * x * (1.0 + jnp.tanh(c * (x + 0.044715 * (x * x * x))))


def _dot(a, b):
    return jnp.dot(a, b, preferred_element_type=f32)


def _dot_nt(a, b):
    return lax.dot_general(a, b, (((1,), (1,)), ((), ())), preferred_element_type=f32)


def _dot_tn(a, b):
    return lax.dot_general(a, b, (((0,), (0,)), ((), ())), preferred_element_type=f32)


def _dot_f32(a, b):
    return jnp.dot(a, b, preferred_element_type=f32, precision=lax.Precision.HIGHEST)


def _norm_matmul_kernel(x_ref, g_ref, w_ref, o_ref, h_ref):
    @pl.when(pl.program_id(1) == 0)
    def _():
        x = x_ref[...]
        ms = jnp.mean(x * x, axis=-1, keepdims=True)
        h_ref[...] = (x * lax.rsqrt(ms + EPS) * g_ref[...]).astype(bf16)

    o_ref[...] = _dot(h_ref[...], w_ref[...])


def _norm_matmul(x, g, w, tm, tn):
    m, d = x.shape
    n = w.shape[1]
    return pl.pallas_call(
        _norm_matmul_kernel,
        grid=(m // tm, n // tn),
        in_specs=[pl.BlockSpec((tm, d), lambda i, j: (i, 0)),
                  pl.BlockSpec((1, d), lambda i, j: (0, 0)),
                  pl.BlockSpec((d, tn), lambda i, j: (0, j))],
        out_specs=pl.BlockSpec((tm, tn), lambda i, j: (i, j)),
        out_shape=jax.ShapeDtypeStruct((m, n), f32),
        scratch_shapes=[pltpu.VMEM((tm, d), bf16)],
        compiler_params=_cparams(2),
        name="norm_in_proj",
    )(x, g, w)


def _band_bias(mode, dist, slope):
    distf = dist.astype(f32)
    if mode == "dilated":
        mult = jnp.zeros(dist.shape, f32)
        for w, d in A_PATTERNS:
            hit = (dist <= w) & ((dist & (d - 1)) == 0)
            mult = mult + jnp.where(hit, 1.0, 0.0)
        logm = jnp.where(mult > 2.5, math.log(3.0), jnp.where(mult > 1.5, math.log(2.0), 0.0))
        return jnp.where((dist >= 0) & (mult > 0.5), logm - slope * distf, NEG)
    return jnp.where((dist >= 0) & (dist < WIN_C), -slope * distf, NEG)


def _band_attn_kernel(slope_ref, sink_ref, q_ref, k_ref, v_ref, o_ref, bias_ref, s_ref, p_ref, k16_ref, v16_ref, *,
                      tq, nd, nt, head_a, head_t, head_b, mode, use_sink, scale, variants):
    hp = pl.program_id(0)
    b = pl.program_id(1)
    qi = pl.program_id(2)
    ng = 2 * nt
    strip = 64
    head = lambda tt, hh: hp * head_a + tt * head_t + hh * head_b

    @pl.when((b == 0) & (qi == 0))
    def _():
        row = lax.broadcasted_iota(jnp.int32, (tq, tq), 0)
        col = lax.broadcasted_iota(jnp.int32, (tq, tq), 1)
        for tt in range(nt):
            for hh in range(2):
                rows = slice((2 * tt + hh) * tq, (2 * tt + hh + 1) * tq)
                slope = slope_ref[head(tt, hh)]
                for d in range(nd):
                    bias = _band_bias(mode, d * tq + row - col, slope)
                    bias_ref[d, rows, :] = jnp.where(bias > 0.5 * NEG, bias * LOG2E, NEG)
        bias_ref[nd] = jnp.full((ng * tq, tq), NEG, f32)

    @pl.when(qi == 0)
    def _():
        k16_ref[...] = k_ref[...].astype(bf16)
        v16_ref[...] = v_ref[...].astype(bf16)

    lane = lax.broadcasted_iota(jnp.int32, (tq, LANE), 1)
    lo = lane < LANE // 2
    blocks = []
    for tt in range(nt):
        q = q_ref[:, tt * LANE:(tt + 1) * LANE] * (scale * LOG2E)
        blocks += [jnp.where(lo, q, 0.0), jnp.where(lo, 0.0, q)]
    qs = jnp.concatenate(blocks, axis=0).astype(bf16)

    def attend(n):
        w0 = jnp.maximum(qi - (n - 1), 0)
        for c in range(n):
            delta = qi - (w0 + c)
            bidx = jnp.where((delta >= 0) & (delta < nd), delta, nd)
            kt = k16_ref[pl.ds(pl.multiple_of((w0 + c) * tq, tq), tq), :]
            s_ref[:, c * tq:(c + 1) * tq] = _dot_nt(qs, kt) + bias_ref[bidx]
        nblk = n * tq // LANE
        ls = []
        for r in range(ng * tq // strip):
            rows = slice(r * strip, (r + 1) * strip)
            mx = s_ref[rows, 0:LANE]
            for j in range(1, nblk):
                mx = jnp.maximum(mx, s_ref[rows, j * LANE:(j + 1) * LANE])
            m = jnp.max(mx, axis=-1, keepdims=True)
            if use_sink:
                g = r * strip // tq
                sink = jnp.full((strip, 1), sink_ref[head(g // 2, g % 2)], f32) * LOG2E
                m = jnp.maximum(m, sink)
            lsum = jnp.zeros((strip, LANE), f32)
            for j in range(nblk):
                p = jnp.exp2(s_ref[rows, j * LANE:(j + 1) * LANE] - m)
                lsum = lsum + p
                p_ref[rows, j * LANE:(j + 1) * LANE] = p.astype(bf16)
            l = jnp.sum(lsum, axis=-1, keepdims=True)
            if use_sink:
                l = l + jnp.exp2(sink - m)
            ls.append(l)
        vt = v16_ref[pl.ds(pl.multiple_of(w0 * tq, tq), n * tq), :]
        o = _dot(p_ref[:, 0:n * tq], vt) / jnp.concatenate(ls, axis=0)
        for tt in range(nt):
            o_ref[:, tt * LANE:(tt + 1) * LANE] = jnp.where(lo, o[2 * tt * tq:(2 * tt + 1) * tq],
                                                            o[(2 * tt + 1) * tq:(2 * tt + 2) * tq])

    prev = 0
    for n in variants:
        cond = (qi >= prev) if n == variants[-1] else ((qi >= prev) & (qi < n))
        pl.when(cond)(functools.partial(attend, n))
        prev = n


def _band_attn(proj, slopes, sinks, *, mode, tq):
    bsz, t, _ = proj.shape
    if mode == "dilated":
        nd = min(WIN_A // tq + 1, t // tq)
        nhp, nt = 4, 1
        head_a, head_t, head_b = 2, 0, 1
        qcol, kcol, vcol = COL_QA, COL_KA, COL_VA
        k_map = lambda hp, b, qi: (b, 0, kcol + hp)
        v_map = lambda hp, b, qi: (b, 0, vcol + hp)
        use_sink, hd = False, HD_A
    else:
        nd = min(-(-WIN_C // tq) + 1, t // tq)
        nhp, nt = 1, 4
        head_a, head_t, head_b = 0, 1, 4
        qcol, kcol, vcol = COL_QC // 4, COL_KC, COL_VC
        k_map = lambda hp, b, qi: (b, 0, kcol)
        v_map = lambda hp, b, qi: (b, 0, vcol)
        use_sink, hd = True, HD_C
    variants = tuple(sorted({-(-nd // 3), -(-2 * nd // 3), nd}))
    kern = functools.partial(_band_attn_kernel, tq=tq, nd=nd, nt=nt, head_a=head_a, head_t=head_t, head_b=head_b,
                             mode=mode, use_sink=use_sink, scale=hd ** -0.5, variants=variants)
    smem = pl.BlockSpec(memory_space=pltpu.SMEM)
    return pl.pallas_call(
        kern,
        grid=(nhp, bsz, t // tq),
        in_specs=[smem, smem,
                  pl.BlockSpec((None, tq, nt * LANE), lambda hp, b, qi: (b, qi, qcol + hp)),
                  pl.BlockSpec((None, t, LANE), k_map),
                  pl.BlockSpec((None, t, LANE), v_map)],
        out_specs=pl.BlockSpec((None, tq, nt * LANE), lambda hp, b, qi: (b, qi, hp)),
        out_shape=jax.ShapeDtypeStruct((bsz, t, 4 * LANE), f32),
        scratch_shapes=[pltpu.VMEM((nd + 1, 2 * nt * tq, tq), f32), pltpu.VMEM((2 * nt * tq, nd * tq), f32),
                        pltpu.VMEM((2 * nt * tq, nd * tq), bf16), pltpu.VMEM((t, LANE), bf16),
                        pltpu.VMEM((t, LANE), bf16)],
        compiler_params=_cparams(3),
        name="band_attn_" + mode,
    )(slopes, sinks, proj, proj, proj)


def _attn_a_sample_kernel(bias_ref, q_ref, kn_ref, vn_ref, kt_ref, vt_ref, o_ref, *, scale):
    nh, hd, p = kt_ref.shape
    for h in range(nh):
        qc = q_ref[h] * scale
        s_new = jnp.sum(qc * kn_ref[h], axis=0, keepdims=True)[:, 0:1]
        s = jnp.concatenate(
            [jnp.sum(kt_ref[h, :, j * LANE:(j + 1) * LANE] * qc, axis=0, keepdims=True) for j in range(p // LANE)],
            axis=1) + bias_ref[h:h + 1, :]
        m = jnp.maximum(jnp.max(s, axis=-1, keepdims=True), s_new)
        pr = jnp.exp(s - m)
        p_new = float(len(A_PATTERNS)) * jnp.exp(s_new - m)
        l = jnp.sum(pr, axis=-1, keepdims=True) + p_new
        acc = vt_ref[h, :, 0:LANE] * pr[:, 0:LANE]
        for j in range(1, p // LANE):
            acc = acc + vt_ref[h, :, j * LANE:(j + 1) * LANE] * pr[:, j * LANE:(j + 1) * LANE]
        o = (jnp.sum(acc, axis=-1, keepdims=True) + p_new * vn_ref[h]) / l
        o_ref[h] = o


def _attn_a_sample(layer, q, kn, vn, cache_kt, cache_vt, slopes):
    _, s, h, hd, p = cache_kt.shape
    dist = (p - np.arange(p)).astype(np.int64)
    mult = sum(((dist <= wd) & (dist % dl == 0)).astype(np.float64) for wd, dl in A_PATTERNS)
    slopes = np.asarray(slopes, np.float32)
    bias = np.where(mult[None] > 0, np.log(np.maximum(mult, 1.0))[None].astype(np.float32)
                    - slopes[:, None] * dist[None].astype(np.float32), np.float32(NEG)).astype(np.float32)
    tok = pl.BlockSpec((None, h, hd, LANE), lambda i: (i, 0, 0, 0))
    cache = pl.BlockSpec((None, None, h, hd, p), lambda i: (layer, i, 0, 0, 0))
    return pl.pallas_call(
        functools.partial(_attn_a_sample_kernel, scale=hd ** -0.5),
        grid=(s,),
        in_specs=[pl.BlockSpec(bias.shape, lambda i: (0, 0)), tok, tok, tok, cache, cache],
        out_specs=tok,
        out_shape=jax.ShapeDtypeStruct((s, h, hd, LANE), f32),
        compiler_params=_cparams(1),
        name="attn_a_sample",
    )(jnp.asarray(bias), q, kn, vn, cache_kt, cache_vt)


def _attn_c_sample_kernel(slope_ref, sink_ref, q_ref, kn_ref, vn_ref, k_ref, v_ref, o_ref, *, bs, scale):
    rows = k_ref.shape[1]
    prow = lax.broadcasted_iota(jnp.int32, (H_C, LANE), 0)
    plane = lax.broadcasted_iota(jnp.int32, (H_C, LANE), 1)
    halfmask = (plane // HD_C) == (prow % 2)
    j = lax.broadcasted_iota(jnp.int32, (H_C, rows), 1)
    dist = rows - j
    slope = slope_ref[...]
    sink = sink_ref[...]
    bias = jnp.where(dist < WIN_C, -slope * dist.astype(f32), NEG)
    for b in range(bs):
        tiles = []
        for t in range(WC_Q // LANE):
            qt = q_ref[b:b + 1, t * LANE:(t + 1) * LANE] * scale
            tiles += [qt, qt]
        qblk = jnp.where(halfmask, jnp.concatenate(tiles, axis=0), 0.0)
        kn = kn_ref[b:b + 1, :]
        vn = vn_ref[b:b + 1, :]
        s_new = jnp.sum(qblk * kn, axis=-1, keepdims=True)
        s = _dot_nt(qblk.astype(bf16), k_ref[b].astype(bf16)) + bias
        m = jnp.maximum(jnp.maximum(s_new, jnp.max(s, axis=-1, keepdims=True)), sink)
        p = jnp.exp(s - m)
        p_new = jnp.exp(s_new - m)
        l = jnp.sum(p, axis=-1, keepdims=True) + p_new + jnp.exp(sink - m)
        acc = _dot(p.astype(bf16), v_ref[b].astype(bf16)) + p_new * vn
        o = jnp.where(halfmask, acc / l, 0.0)
        for t in range(WC_Q // LANE):
            o_ref[b:b + 1, t * LANE:(t + 1) * LANE] = o[2 * t:2 * t + 1, :] + o[2 * t + 1:2 * t + 2, :]


def _attn_c_sample(proj_s, cache_k, cache_v, slopes, sinks, bs):
    s, p, w = cache_k.shape
    assert p == WIN_C and w == LANE
    col = pl.BlockSpec((H_C, 1), lambda i: (0, 0))
    cache = pl.BlockSpec((bs, p, w), lambda i: (i, 0, 0))
    return pl.pallas_call(
        functools.partial(_attn_c_sample_kernel, bs=bs, scale=HD_C ** -0.5),
        grid=(s // bs,),
        in_specs=[col, col,
                  pl.BlockSpec((bs, WC_Q), lambda i: (i, COL_QC * LANE // WC_Q)),
                  pl.BlockSpec((bs, LANE), lambda i: (i, COL_KC)),
                  pl.BlockSpec((bs, LANE), lambda i: (i, COL_VC)),
                  cache, cache],
        out_specs=pl.BlockSpec((bs, WC_Q), lambda i: (i, 0)),
        out_shape=jax.ShapeDtypeStruct((s, WC_Q), f32),
        compiler_params=_cparams(1),
        name="attn_c_sample",
    )(slopes, sinks, proj_s, proj_s, proj_s, cache_k, cache_v)


def _gdn_gates(ab, gp_ref):
    g = -jnp.exp(gp_ref[0:1, :]) * _softplus(ab + gp_ref[1:2, :])
    return g, _sigmoid(ab)


def _l2norm(x):
    return x * lax.rsqrt(jnp.sum(x * x, axis=-1, keepdims=True) + EPS)


def _gated_out(o, z, gn):
    y = o * lax.rsqrt(jnp.mean(o * o, axis=-1, keepdims=True) + EPS) * gn
    return y * _silu(z)


def _gdn_prompt_kernel(x_ref, z_ref, ab_ref, wconv_ref, gp_ref, gn_ref, o_ref, s_ref, buf_ref, act_ref, *, nb, c):
    ti = pl.program_id(0)
    nch = nb * H_B

    @pl.when(ti == 0)
    def _():
        buf_ref[:, 0:8, :] = jnp.zeros((nb, 8, CB_CH), f32)
        s_ref[...] = jnp.zeros(s_ref.shape, f32)

    for b in range(nb):
        buf_ref[b, 8:8 + c, :] = x_ref[b]
        for cb in range(CB_CH // LANE):
            sl = slice(cb * LANE, (cb + 1) * LANE)
            y = buf_ref[b, pl.ds(8 - (CONV_B - 1), c), sl] * wconv_ref[0:1, sl]
            for jw in range(1, CONV_B):
                y = y + buf_ref[b, pl.ds(8 - (CONV_B - 1) + jw, c), sl] * wconv_ref[jw:jw + 1, sl]
            act_ref[b, :, sl] = _silu(y)
        buf_ref[b, 0:8, :] = buf_ref[b, c:c + 8, :]

    g, beta = _gdn_gates(ab_ref[...].reshape(nb * c, LANE), gp_ref)
    rowc = lax.broadcasted_iota(jnp.int32, (nb * c, LANE), 0) & (c - 1)
    sh = 1
    while sh < c:
        g = g + jnp.where(rowc >= sh, pltpu.roll(g, sh, axis=0), 0.0)
        sh *= 2

    qs, ks, vs, zs, gcols, grows, bcols = [], [], [], [], [], [], []
    for b in range(nb):
        gb = g[b * c:(b + 1) * c]
        gb_t = jnp.transpose(gb)
        for h in range(H_B):
            qs.append(_l2norm(act_ref[b, :, h * DK_B:(h + 1) * DK_B]) * (DK_B ** -0.5))
            ks.append(_l2norm(act_ref[b, :, WB_QK + h * DK_B:WB_QK + (h + 1) * DK_B]))
            vs.append(act_ref[b, :, 2 * WB_QK + h * DV_B:2 * WB_QK + (h + 1) * DV_B])
            zs.append(z_ref[b, :, h * DV_B:(h + 1) * DV_B])
            gcols.append(gb[:, h:h + 1])
            grows.append(gb_t[h:h + 1, :])
            bcols.append(beta[b * c:(b + 1) * c, H_B + h:H_B + h + 1])
    q, k, v, z = jnp.stack(qs), jnp.stack(ks), jnp.stack(vs), jnp.stack(zs)
    gcol, grow, bcol = jnp.stack(gcols), jnp.stack(grows), jnp.stack(bcols)

    ri = lax.broadcasted_iota(jnp.int32, (c, c), 0)
    ci = lax.broadcasted_iota(jnp.int32, (c, c), 1)
    incl = (ci <= ri)[None]
    strict = (ci < ri)[None]
    eye = jnp.where(ci == ri, 1.0, 0.0)[None]
    bmm = lambda a, b_: jnp.einsum("nij,njk->nik", a.astype(bf16), b_.astype(bf16), preferred_element_type=f32)
    bmm_nt = lambda a, b_: jnp.einsum("nik,njk->nij", a.astype(bf16), b_.astype(bf16), preferred_element_type=f32)

    decay = jnp.exp(jnp.where(incl, gcol - grow, NEG))
    eg = jnp.exp(gcol)
    kb = k * bcol
    a_low = jnp.where(strict, bmm_nt(kb, k) * decay, 0.0)
    blk = min(16, c)
    rb, cb_ = ri // blk, ci // blk
    pw = jnp.where((rb == cb_)[None], -a_low, 0.0)
    tmat = eye + pw
    for _ in range(int(math.log2(blk)) - 1):
        pw = bmm(pw, pw)
        tmat = tmat + bmm(tmat, pw)
    while blk < c:
        off = ((ri // (2 * blk)) == (ci // (2 * blk))) & (rb != cb_)
        tmat = tmat - bmm(tmat, bmm(jnp.where(off[None], a_low, 0.0), tmat))
        blk *= 2
        rb, cb_ = ri // blk, ci // blk
    u = bmm(tmat, v * bcol)
    w = bmm(tmat, kb * eg)
    qk = jnp.where(incl, bmm_nt(q, k) * decay, 0.0)
    s = s_ref[...].reshape(nch, DK_B, DV_B)
    v_new = u - bmm(w, s)
    o = bmm(q * eg, s) + bmm(qk, v_new)
    glast = gcol[:, c - 1:c, :]
    k_dec = (k * jnp.exp(glast - gcol)).astype(bf16)
    vb = v_new.astype(bf16)
    upd = jnp.stack([_dot_tn(k_dec[n], vb[n]) for n in range(nch)])
    s_ref[...] = (s * jnp.exp(glast) + upd).reshape(nb, H_B, DK_B, DV_B)
    y = _gated_out(o, z, gn_ref[...][None])
    for b in range(nb):
        for h in range(H_B):
            o_ref[b, :, h * DV_B:(h + 1) * DV_B] = y[b * H_B + h]


def _gdn_prompt(proj, w_conv, gate_params, gdn_norm, c):
    bsz, t, _ = proj.shape
    const = lambda shape: pl.BlockSpec(shape, lambda ti: (0,) * len(shape))
    return pl.pallas_call(
        functools.partial(_gdn_prompt_kernel, nb=bsz, c=c),
        grid=(t // c,),
        in_specs=[pl.BlockSpec((bsz, c, CB_CH), lambda ti: (0, ti, COL_CB * LANE // CB_CH)),
                  pl.BlockSpec((bsz, c, WB_V), lambda ti: (0, ti, COL_ZB * LANE // WB_V)),
                  pl.BlockSpec((bsz, c, LANE), lambda ti: (0, ti, COL_AB)),
                  const((CONV_B, CB_CH)), const((8, LANE)), const((1, DV_B))],
        out_specs=[pl.BlockSpec((bsz, c, WB_V), lambda ti: (0, ti, 0)),
                   pl.BlockSpec((bsz, H_B, DK_B, DV_B), lambda ti: (0, 0, 0, 0))],
        out_shape=[jax.ShapeDtypeStruct((bsz, t, WB_V), f32),
                   jax.ShapeDtypeStruct((bsz, H_B, DK_B, DV_B), f32)],
        scratch_shapes=[pltpu.VMEM((bsz, c + 8, CB_CH), f32), pltpu.VMEM((bsz, c, CB_CH), f32)],
        compiler_params=_cparams(1),
        name="gdn_prompt",
    )(proj, proj, proj, w_conv, gate_params, gdn_norm)


def _gdn_sample_kernel(x_ref, c0_ref, c1_ref, c2_ref, z_ref, ab_ref, s_ref, wconv_ref, gp_ref, gn_ref,
                       o_ref, so_ref, *, bs):
    y = (c0_ref[...] * wconv_ref[0:1, :] + c1_ref[...] * wconv_ref[1:2, :] + c2_ref[...] * wconv_ref[2:3, :]
         + x_ref[...] * wconv_ref[3:4, :])
    act = _silu(y)
    g, beta = _gdn_gates(ab_ref[...], gp_ref)
    eg = jnp.exp(g)
    gn = gn_ref[...]
    rowi = lax.broadcasted_iota(jnp.int32, (bs, DK_B), 0)
    for h in range(H_B):
        q = _l2norm(act[:, h * DK_B:(h + 1) * DK_B]) * (DK_B ** -0.5)
        k = _l2norm(act[:, WB_QK + h * DK_B:WB_QK + (h + 1) * DK_B])
        v = act[:, 2 * WB_QK + h * DV_B:2 * WB_QK + (h + 1) * DV_B]
        egc = eg[:, h:h + 1]
        bc = beta[:, H_B + h:H_B + h + 1]
        qg = q * egc
        ks_rows, qs_rows = [], []
        for b in range(bs):
            lhs = jnp.where(rowi == 0, jnp.broadcast_to(k[b:b + 1], (bs, DK_B)),
                            jnp.where(rowi == 1, jnp.broadcast_to(qg[b:b + 1], (bs, DK_B)), 0.0))
            r = _dot(lhs.astype(bf16), s_ref[b, h].astype(bf16))
            ks_rows.append(r[0:1])
            qs_rows.append(r[1:2])
        ks = jnp.concatenate(ks_rows, axis=0)
        qs = jnp.concatenate(qs_rows, axis=0)
        v_new = bc * v - (bc * egc) * ks
        qk = jnp.sum(q * k, axis=-1, keepdims=True)
        o = qs + qk * v_new
        o_ref[:, h * DV_B:(h + 1) * DV_B] = _gated_out(o, z_ref[:, h * DV_B:(h + 1) * DV_B], gn)
        vb = v_new.astype(bf16)
        for b in range(bs):
            km = jnp.where(rowi == b, k, 0.0).astype(bf16)
            so_ref[b, h] = s_ref[b, h] * egc[b:b + 1, :] + _dot_tn(km, vb)


def _gdn_sample(layer, proj_s, conv_state, ssm_state, w_conv, gate_params, gdn_norm, bs):
    s = proj_s.shape[0]
    const = lambda shape: pl.BlockSpec(shape, lambda i: (0,) * len(shape))
    cst = lambda j: pl.BlockSpec((bs, CB_CH), lambda i: (i, j))
    st = pl.BlockSpec((bs, H_B, DK_B, DV_B), lambda i: (i, 0, 0, 0))
    st_in = pl.BlockSpec((None, bs, H_B, DK_B, DV_B), lambda i: (layer, i, 0, 0, 0))
    return pl.pallas_call(
        functools.partial(_gdn_sample_kernel, bs=bs),
        grid=(s // bs,),
        in_specs=[pl.BlockSpec((bs, CB_CH), lambda i: (i, COL_CB * LANE // CB_CH)),
                  cst(0), cst(1), cst(2),
                  pl.BlockSpec((bs, WB_V), lambda i: (i, COL_ZB * LANE // WB_V)),
                  pl.BlockSpec((bs, LANE), lambda i: (i, COL_AB)),
                  st_in, const((CONV_B, CB_CH)), const((8, LANE)), const((1, DV_B))],
        out_specs=[pl.BlockSpec((bs, WB_V), lambda i: (i, 0)), st],
        out_shape=[jax.ShapeDtypeStruct((s, WB_V), f32),
                   jax.ShapeDtypeStruct(ssm_state.shape[1:], f32)],
        compiler_params=_cparams(1),
        name="gdn_sample",
    )(proj_s, conv_state, conv_state, conv_state, proj_s, proj_s, ssm_state, w_conv, gate_params, gdn_norm)


def _merge_kernel(x_ref, g_ref, oa_ref, ob_ref, oc_ref, wa_ref, wb_ref, wc_ref, wo_ref, y_ref):
    d = D_MODEL
    m = _sigmoid(g_ref[:, 0:d]) * _dot(oa_ref[...].astype(bf16), wa_ref[...])
    m = m + _sigmoid(g_ref[:, d:2 * d]) * _dot(ob_ref[...].astype(bf16), wb_ref[...])
    m = m + _sigmoid(g_ref[:, 2 * d:3 * d]) * _dot(oc_ref[...].astype(bf16), wc_ref[...])
    y_ref[...] = x_ref[...] + _dot(m.astype(bf16), wo_ref[...])


def _merge(x, proj, o_a, o_b, o_c, wa, wb, wc, wo, tm):
    m, d = x.shape
    row = lambda w: pl.BlockSpec((tm, w), lambda i: (i, 0))
    const = lambda a: pl.BlockSpec(a.shape, lambda i: (0, 0))
    return pl.pallas_call(
        _merge_kernel,
        grid=(m // tm,),
        in_specs=[row(d), row(3 * d), row(WA), row(WB_V), row(WC_Q), const(wa), const(wb), const(wc), const(wo)],
        out_specs=row(d),
        out_shape=jax.ShapeDtypeStruct((m, d), f32),
        compiler_params=_cparams(1),
        name="merge_out_proj",
    )(x, proj, o_a, o_b, o_c, wa, wb, wc, wo)


def _ffn_prompt_kernel(x_ref, g_ref, wg_ref, wu_ref, cg_ref, cu_ref, bg_ref, bu_ref, wd_ref,
                       y_ref, sg_ref, su_ref, h_ref, bufg_ref, bufu_ref, carg_ref, caru_ref, *, tm):
    ti = pl.program_id(1)
    j = pl.program_id(2)
    kw = CONV_FFN

    @pl.when(j == 0)
    def _():
        x = x_ref[...]
        ms = jnp.mean(x * x, axis=-1, keepdims=True)
        h_ref[...] = (x * lax.rsqrt(ms + EPS) * g_ref[...]).astype(bf16)
        y_ref[...] = x

    @pl.when(ti == 0)
    def _():
        carg_ref[j] = jnp.zeros(carg_ref.shape[1:], f32)
        caru_ref[j] = jnp.zeros(caru_ref.shape[1:], f32)

    def conv(w_ref, cw_ref, b_ref, buf_ref, car_ref, st_ref):
        buf_ref[0:8, :] = car_ref[j]
        buf_ref[8:8 + tm, :] = _dot(h_ref[...], w_ref[...])
        y = b_ref[...] + buf_ref[pl.ds(8 - (kw - 1), tm), :] * cw_ref[0:1, :]
        for jw in range(1, kw):
            y = y + buf_ref[pl.ds(8 - (kw - 1) + jw, tm), :] * cw_ref[jw:jw + 1, :]
        car_ref[j] = buf_ref[tm:tm + 8, :]
        st_ref[...] = buf_ref[tm + 8 - (kw - 1):tm + 8, :]
        return y

    gate = conv(wg_ref, cg_ref, bg_ref, bufg_ref, carg_ref, sg_ref)
    up = conv(wu_ref, cu_ref, bu_ref, bufu_ref, caru_ref, su_ref)
    act = (_gelu_tanh(gate) * up).astype(bf16)
    y_ref[...] += _dot(act, wd_ref[...])


def _ffn_prompt(x, g, w_up, w_conv, b_conv, w_down, tm, tf):
    bsz, t, d = x.shape
    nj = D_FF // tf
    kw = CONV_FFN
    gcol = lambda shape: pl.BlockSpec(shape, lambda b, ti, j: (0, j))
    ucol = lambda shape: pl.BlockSpec(shape, lambda b, ti, j: (0, nj + j))
    y, sg, su = pl.pallas_call(
        functools.partial(_ffn_prompt_kernel, tm=tm),
        grid=(bsz, t // tm, nj),
        in_specs=[pl.BlockSpec((None, tm, d), lambda b, ti, j: (b, ti, 0)),
                  pl.BlockSpec((1, d), lambda b, ti, j: (0, 0)),
                  gcol((d, tf)), ucol((d, tf)), gcol((kw, tf)), ucol((kw, tf)), gcol((1, tf)), ucol((1, tf)),
                  pl.BlockSpec((tf, d), lambda b, ti, j: (j, 0))],
        out_specs=[pl.BlockSpec((None, tm, d), lambda b, ti, j: (b, ti, 0)),
                   pl.BlockSpec((None, None, kw - 1, tf), lambda b, ti, j: (b, ti, 0, j)),
                   pl.BlockSpec((None, None, kw - 1, tf), lambda b, ti, j: (b, ti, 0, j))],
        out_shape=[jax.ShapeDtypeStruct((bsz, t, d), f32),
                   jax.ShapeDtypeStruct((bsz, t // tm, kw - 1, D_FF), f32),
                   jax.ShapeDtypeStruct((bsz, t // tm, kw - 1, D_FF), f32)],
        scratch_shapes=[pltpu.VMEM((tm, d), bf16),
                        pltpu.VMEM((tm + 8, tf), f32), pltpu.VMEM((tm + 8, tf), f32),
                        pltpu.VMEM((nj, 8, tf), f32), pltpu.VMEM((nj, 8, tf), f32)],
        compiler_params=_cparams(3),
        name="conv_ffn_prompt",
    )(x, g, w_up, w_up, w_conv, w_conv, b_conv, b_conv, w_down)
    return y, jnp.concatenate([sg[:, -1], su[:, -1]], axis=-1)


def _ffn_sample_kernel(x_ref, g_ref, wg_ref, wu_ref, cg_ref, cu_ref, bg_ref, bu_ref, wd_ref,
                       pg0_ref, pg1_ref, pu0_ref, pu1_ref, y_ref, ug_ref, uu_ref, h_ref):
    @pl.when(pl.program_id(0) == 0)
    def _():
        x = x_ref[...]
        ms = jnp.mean(x * x, axis=-1, keepdims=True)
        h_ref[...] = (x * lax.rsqrt(ms + EPS) * g_ref[...]).astype(bf16)
        y_ref[...] = x

    def conv(w_ref, cw_ref, b_ref, p0_ref, p1_ref, u_ref):
        u = _dot(h_ref[...], w_ref[...])
        u_ref[...] = u
        return b_ref[...] + p0_ref[...] * cw_ref[0:1, :] + p1_ref[...] * cw_ref[1:2, :] + u * cw_ref[2:3, :]

    gate = conv(wg_ref, cg_ref, bg_ref, pg0_ref, pg1_ref, ug_ref)
    up = conv(wu_ref, cu_ref, bu_ref, pu0_ref, pu1_ref, uu_ref)
    y_ref[...] += _dot((_gelu_tanh(gate) * up).astype(bf16), wd_ref[...])


def _ffn_sample(x, g, w_up, w_conv, b_conv, w_down, state, tf):
    s, d = x.shape
    nj = D_FF // tf
    kw = CONV_FFN
    assert kw == 3
    col = lambda shape, off: pl.BlockSpec(shape, lambda j: (0, off + j))
    y, ug, uu = pl.pallas_call(
        _ffn_sample_kernel,
        grid=(nj,),
        in_specs=[pl.BlockSpec((s, d), lambda j: (0, 0)), pl.BlockSpec((1, d), lambda j: (0, 0)),
                  col((d, tf), 0), col((d, tf), nj), col((kw, tf), 0), col((kw, tf), nj),
                  col((1, tf), 0), col((1, tf), nj),
                  pl.BlockSpec((tf, d), lambda j: (j, 0)),
                  col((s, tf), 0), col((s, tf), 2 * nj), col((s, tf), nj), col((s, tf), 3 * nj)],
        out_specs=[pl.BlockSpec((s, d), lambda j: (0, 0)), col((s, tf), 0), col((s, tf), 0)],
        out_shape=[jax.ShapeDtypeStruct((s, d), f32),
                   jax.ShapeDtypeStruct((s, D_FF), f32), jax.ShapeDtypeStruct((s, D_FF), f32)],
        scratch_shapes=[pltpu.VMEM((s, d), bf16)],
        compiler_params=_cparams(1),
        name="conv_ffn_sample",
    )(x, g, w_up, w_up, w_conv, w_conv, b_conv, b_conv, w_down, state, state, state, state)
    return y, jnp.concatenate([ug, uu], axis=-1)


def _final_norm_kernel(x_ref, g_ref, y_ref):
    x = x_ref[...]
    ms = jnp.mean(x * x, axis=-1, keepdims=True)
    y_ref[...] = x * lax.rsqrt(ms + EPS) * g_ref[...]


def _final_norm(x, g, tm):
    m, d = x.shape
    return pl.pallas_call(
        _final_norm_kernel,
        grid=(m // tm,),
        in_specs=[pl.BlockSpec((tm, d), lambda i: (i, 0)), pl.BlockSpec((1, d), lambda i: (0, 0))],
        out_specs=pl.BlockSpec((tm, d), lambda i: (i, 0)),
        out_shape=jax.ShapeDtypeStruct((m, d), f32),
        compiler_params=_cparams(1),
        name="final_norm",
    )(x, g)


def _pack_w_in(w_in):
    splits = np.cumsum([WA, WA, WA, CB_CH, WB_V, H_B, H_B, WC_Q, WC_KV, WC_KV])
    qa, ka, va, cb, zb, adec, bgate, qc, kc, vc, gates = jnp.split(w_in, splits.tolist(), axis=-1)
    lead = w_in.shape[:-1]
    qc = qc.reshape(lead + (H_C, HD_C))[..., list(C_HEAD_ORDER), :].reshape(lead + (WC_Q,))
    ab = jnp.concatenate([adec, bgate, jnp.zeros(lead + (LANE - 2 * H_B,), w_in.dtype)], axis=-1)
    pad = jnp.zeros(lead + (LANE,), w_in.dtype)
    packed = jnp.concatenate([gates, qa, ka, va, cb, zb, qc, ab, kc, vc, pad], axis=-1)
    assert packed.shape[-1] == N_COLS
    return packed.astype(bf16)


def _pick(n, pref):
    for t in pref:
        if n % t == 0:
            return t
    return n


def kernel(x_prompt, x_sample, cache_a_k, cache_a_v, cache_c_k, cache_c_v, state_b_conv, state_b_ssm, state_ffn_conv, norm_mix, w_in, w_conv_b, a_log, dt_bias, gdn_norm, sinks, w_br_a, w_br_b, w_br_c, w_out, norm_ffn, w_up, w_conv_ffn, b_conv_ffn, w_down, norm_final):
    depth = w_in.shape[0]
    bp, t, d = x_prompt.shape
    ns = x_sample.shape[0]
    assert x_sample.shape[1] == 1 and d == D_MODEL and t % 256 == 0

    slopes_a, slopes_c = _alibi_slopes()
    order = list(C_HEAD_ORDER)
    slopes_c_col = jnp.asarray(slopes_c[order]).reshape(H_C, 1)

    w_in_p = _pack_w_in(w_in)
    w_br_a16 = w_br_a.astype(bf16)
    w_br_b16 = w_br_b.astype(bf16)
    w_br_c16 = w_br_c.reshape(depth, H_C, HD_C, d)[:, order].reshape(depth, WC_Q, d).astype(bf16)
    w_out16 = w_out.astype(bf16)
    w_up16 = w_up.astype(bf16)
    w_down16 = w_down.astype(bf16)
    gate_params = jnp.zeros((depth, 8, LANE), f32)
    gate_params = gate_params.at[:, 0, :H_B].set(a_log.astype(f32)).at[:, 1, :H_B].set(dt_bias.astype(f32))

    tm_p = _pick(bp * t, (1024, 512, 256))
    tm_ffn = _pick(t, (512, 256))
    tn = _pick(N_COLS, (512,))
    tf = 512
    bs = _pick(ns, (8,))
    cache_a_kt = jnp.transpose(cache_a_k, (0, 1, 3, 4, 2))
    cache_a_vt = jnp.transpose(cache_a_v, (0, 1, 3, 4, 2))

    xp = x_prompt.reshape(bp * t, d)
    xs = x_sample.reshape(ns, d)
    outs = {k: [] for k in ("akp", "avp", "ckp", "cvp", "bcp", "bsp", "fcp", "aks", "avs", "cks", "cvs", "bcs",
                            "bss", "fcs")}
    ra = min(WIN_A, t)
    rc = min(WIN_C, t)
    for l in range(depth):
        g_mix = norm_mix[l].reshape(1, d)
        sinks_l = sinks[l].astype(f32)
        proj = _norm_matmul(xp, g_mix, w_in_p[l], tm_p, tn)
        proj3 = proj.reshape(bp, t, N_COLS)
        o_a = _band_attn(proj3, jnp.asarray(slopes_a), sinks_l, mode="dilated", tq=256)
        o_c = _band_attn(proj3, jnp.asarray(slopes_c), sinks_l, mode="window", tq=128)
        o_b, ssm_p = _gdn_prompt(proj3, w_conv_b[l], gate_params[l], gdn_norm[l].reshape(1, DV_B), c=128)
        xp = _merge(xp, proj, o_a.reshape(bp * t, WA), o_b.reshape(bp * t, WB_V), o_c.reshape(bp * t, WC_Q),
                    w_br_a16[l], w_br_b16[l], w_br_c16[l], w_out16[l], 256)
        y3, fc_p = _ffn_prompt(xp.reshape(bp, t, d), norm_ffn[l].reshape(1, d), w_up16[l], w_conv_ffn[l],
                               b_conv_ffn[l].reshape(1, 2 * D_FF), w_down16[l], tm_ffn, tf)
        xp = y3.reshape(bp * t, d)
        ka = proj3[:, t - ra:, COL_KA * LANE:COL_KA * LANE + WA]
        va = proj3[:, t - ra:, COL_VA * LANE:COL_VA * LANE + WA]
        outs["akp"].append(ka.reshape(bp, ra, H_A, HD_A))
        outs["avp"].append(va.reshape(bp, ra, H_A, HD_A))
        outs["ckp"].append(proj3[:, t - rc:, COL_KC * LANE:(COL_KC + 1) * LANE].reshape(bp, rc, KVH_C, HD_C))
        outs["cvp"].append(proj3[:, t - rc:, COL_VC * LANE:(COL_VC + 1) * LANE].reshape(bp, rc, KVH_C, HD_C))
        outs["bcp"].append(proj3[:, t - (CONV_B - 1):, COL_CB * LANE:COL_CB * LANE + CB_CH])
        outs["bsp"].append(ssm_p)
        outs["fcp"].append(fc_p)
        proj_s = _norm_matmul(xs, g_mix, w_in_p[l], ns, tn)
        heads_a = lambda col: proj_s[:, col * LANE:col * LANE + WA].reshape(ns, H_A, HD_A)
        ka_s, va_s = heads_a(COL_KA), heads_a(COL_VA)
        lanes = lambda a: jnp.broadcast_to(a[..., None], a.shape + (LANE,))
        o_a = _attn_a_sample(l, lanes(heads_a(COL_QA)), lanes(ka_s), lanes(va_s), cache_a_kt, cache_a_vt,
                             slopes_a)[..., 0].reshape(ns, WA)
        pc = cache_c_k.shape[2]
        o_c = _attn_c_sample(proj_s, cache_c_k[l].reshape(ns, pc, WC_KV), cache_c_v[l].reshape(ns, pc, WC_KV),
                             slopes_c_col, sinks_l[jnp.asarray(order)].reshape(H_C, 1), bs)
        conv_st = state_b_conv[l]
        o_b, ssm_s = _gdn_sample(l, proj_s, conv_st.reshape(ns, (CONV_B - 1) * CB_CH), state_b_ssm, w_conv_b[l],
                                 gate_params[l], gdn_norm[l].reshape(1, DV_B), bs)
        xs = _merge(xs, proj_s, o_a, o_b, o_c, w_br_a16[l], w_br_b16[l], w_br_c16[l], w_out16[l], ns)
        ffn_st = state_ffn_conv[l]
        xs, u_s = _ffn_sample(xs, norm_ffn[l].reshape(1, d), w_up16[l], w_conv_ffn[l],
                              b_conv_ffn[l].reshape(1, 2 * D_FF), w_down16[l],
                              ffn_st.reshape(ns, (CONV_FFN - 1) * 2 * D_FF), tf)
        outs["aks"].append(ka_s[:, None])
        outs["avs"].append(va_s[:, None])
        outs["cks"].append(proj_s[:, COL_KC * LANE:(COL_KC + 1) * LANE].reshape(ns, 1, KVH_C, HD_C))
        outs["cvs"].append(proj_s[:, COL_VC * LANE:(COL_VC + 1) * LANE].reshape(ns, 1, KVH_C, HD_C))
        xb_new = proj_s[:, COL_CB * LANE:COL_CB * LANE + CB_CH]
        outs["bcs"].append(jnp.concatenate([conv_st[:, 1:], xb_new[:, None, :]], axis=1))
        outs["bss"].append(ssm_s)
        outs["fcs"].append(jnp.concatenate([ffn_st[:, 1:], u_s[:, None, :]], axis=1))

    g_fin = norm_final.reshape(1, d)
    y_prompt = _final_norm(xp, g_fin, tm_p).reshape(bp, t, d)
    y_sample = _final_norm(xs, g_fin, ns).reshape(ns, 1, d)
    st = {k: jnp.stack(v) for k, v in outs.items()}
    return (y_prompt, y_sample, st["akp"], st["avp"], st["ckp"], st["cvp"], st["bcp"], st["bsp"], st["fcp"],
            st["aks"], st["avs"], st["cks"], st["cvs"], st["bcs"], st["bss"], st["fcs"])
```

```python
import functools
import math

import numpy as np
import jax
import jax.numpy as jnp
from jax import lax
from jax.experimental import pallas as pl
from jax.experimental.pallas import tpu as pltpu

D_MODEL = 1024
H_A, HD_A = 8, 64
A_PATTERNS = ((128, 1), (512, 4), (2048, 16))
WIN_A = 2048
H_B, DK_B, DV_B = 4, 128, 128
CONV_B = 4
CHUNK_B = 64
H_C, KVH_C, HD_C = 8, 2, 64
WIN_C = 128
D_FF = 3 * D_MODEL
CONV_FFN = 3
EPS = 1e-6
NEG = -1e30
LOG2E = math.log2(math.e)

WA = H_A * HD_A
WB_QK = H_B * DK_B
WB_V = H_B * DV_B
WC_Q = H_C * HD_C
WC_KV = KVH_C * HD_C
CB_CH = 2 * WB_QK + WB_V

LANE = 128
C_HEAD_ORDER = (0, 4, 1, 5, 2, 6, 3, 7)

COL_GATES = 0
COL_QA, COL_KA, COL_VA = 24, 28, 32
COL_CB = 36
COL_ZB = 48
COL_QC = 52
COL_AB = 56
COL_KC, COL_VC = 57, 58
N_COLS = 60 * LANE

VMEM_LIMIT = 48 * 1024 * 1024

f32 = jnp.float32
bf16 = jnp.bfloat16


def _cparams(n_axes):
    return pltpu.CompilerParams(dimension_semantics=("arbitrary",) * n_axes, vmem_limit_bytes=VMEM_LIMIT)


def _alibi_slopes():
    n = H_A + H_C
    s = (2.0 ** (-8.0 * (np.arange(n) + 1) / n)).astype(np.float32)
    return s[H_C:], s[:H_C]


def _sigmoid(x):
    return 1.0 / (1.0 + jnp.exp(-x))


def _silu(x):
    return x * _sigmoid(x)


def _softplus(x):
    return jnp.maximum(x, 0.0) + jnp.log1p(jnp.exp(-jnp.abs(x)))


def _gelu_tanh(x):
    c = math.sqrt(2.0 / math.pi)
    return 0.5 * x * (1.0 + jnp.tanh(c * (x + 0.044715 * (x * x * x))))


def _dot(a, b):
    return jnp.dot(a, b, preferred_element_type=f32)


def _dot_nt(a, b):
    return lax.dot_general(a, b, (((1,), (1,)), ((), ())), preferred_element_type=f32)


def _dot_tn(a, b):
    return lax.dot_general(a, b, (((0,), (0,)), ((), ())), preferred_element_type=f32)


def _dot_f32(a, b):
    return jnp.dot(a, b, preferred_element_type=f32, precision=lax.Precision.HIGHEST)


def _norm_matmul_kernel(x_ref, g_ref, w_ref, o_ref, h_ref):
    @pl.when(pl.program_id(1) == 0)
    def _():
        x = x_ref[...]
        ms = jnp.mean(x * x, axis=-1, keepdims=True)
        h_ref[...] = (x * lax.rsqrt(ms + EPS) * g_ref[...]).astype(bf16)

    o_ref[...] = _dot(h_ref[...], w_ref[...])


def _norm_matmul(x, g, w, tm, tn):
    m, d = x.shape
    n = w.shape[1]
    return pl.pallas_call(
        _norm_matmul_kernel,
        grid=(m // tm, n // tn),
        in_specs=[pl.BlockSpec((tm, d), lambda i, j: (i, 0)),
                  pl.BlockSpec((1, d), lambda i, j: (0, 0)),
                  pl.BlockSpec((d, tn), lambda i, j: (0, j))],
        out_specs=pl.BlockSpec((tm, tn), lambda i, j: (i, j)),
        out_shape=jax.ShapeDtypeStruct((m, n), f32),
        scratch_shapes=[pltpu.VMEM((tm, d), bf16)],
        compiler_params=_cparams(2),
        name="norm_in_proj",
    )(x, g, w)


def _band_bias(patterns, stride, dist, slope):
    tok = dist * stride
    mult = jnp.zeros(dist.shape, f32)
    for w, d in patterns:
        hit = (tok <= w) & ((tok & (d - 1)) == 0)
        mult = mult + jnp.where(hit, 1.0, 0.0)
    logm = jnp.where(mult > 2.5, math.log(3.0), jnp.where(mult > 1.5, math.log(2.0), 0.0))
    return jnp.where((dist >= 0) & (mult > 0.5), logm - slope * tok.astype(f32), NEG)


def _band_attn_kernel(slope_ref, sink_ref, q_ref, k_ref, v_ref, *rest,
                      tq, nd, nt, head_a, head_t, head_b, patterns, stride, use_sink, scale, variants,
                      merge_in, stats_out):
    if merge_in:
        acc_in_ref, m_in_ref, l_in_ref, *rest = rest
    if stats_out:
        o_ref, m_out_ref, l_out_ref, *rest = rest
    else:
        o_ref, *rest = rest
    bias_ref, s_ref, p_ref, k16_ref, v16_ref = rest
    hp = pl.program_id(0)
    b = pl.program_id(1)
    qi = pl.program_id(2)
    ng = 2 * nt
    strip = 64
    head = lambda tt, hh: hp * head_a + tt * head_t + hh * head_b

    @pl.when((b == 0) & (qi == 0))
    def _():
        row = lax.broadcasted_iota(jnp.int32, (tq, tq), 0)
        col = lax.broadcasted_iota(jnp.int32, (tq, tq), 1)
        for tt in range(nt):
            for hh in range(2):
                rows = slice((2 * tt + hh) * tq, (2 * tt + hh + 1) * tq)
                slope = slope_ref[head(tt, hh)]
                for d in range(nd):
                    bias = _band_bias(patterns, stride, d * tq + row - col, slope)
                    bias_ref[d, rows, :] = jnp.where(bias > 0.5 * NEG, bias * LOG2E, NEG)
        bias_ref[nd] = jnp.full((ng * tq, tq), NEG, f32)

    @pl.when(qi == 0)
    def _():
        k16_ref[...] = k_ref[...].astype(bf16)
        v16_ref[...] = v_ref[...].astype(bf16)

    lane = lax.broadcasted_iota(jnp.int32, (tq, LANE), 1)
    lo = lane < LANE // 2
    blocks = []
    for tt in range(nt):
        q = q_ref[:, tt * LANE:(tt + 1) * LANE] * (scale * LOG2E)
        blocks += [jnp.where(lo, q, 0.0), jnp.where(lo, 0.0, q)]
    qs = jnp.concatenate(blocks, axis=0).astype(bf16)

    def attend(n):
        w0 = jnp.maximum(qi - (n - 1), 0)
        for c in range(n):
            delta = qi - (w0 + c)
            bidx = jnp.where((delta >= 0) & (delta < nd), delta, nd)
            kt = k16_ref[pl.ds(pl.multiple_of((w0 + c) * tq, tq), tq), :]
            s_ref[:, c * tq:(c + 1) * tq] = _dot_nt(qs, kt) + bias_ref[bidx]
        nblk = n * tq // LANE
        ls, ms, sinks = [], [], []
        for r in range(ng * tq // strip):
            rows = slice(r * strip, (r + 1) * strip)
            mx = s_ref[rows, 0:LANE]
            for j in range(1, nblk):
                mx = jnp.maximum(mx, s_ref[rows, j * LANE:(j + 1) * LANE])
            m = jnp.broadcast_to(jnp.max(mx, axis=-1, keepdims=True), (strip, LANE))
            if use_sink:
                g = r * strip // tq
                sink = jnp.full((strip, LANE), sink_ref[head(g // 2, g % 2)], f32) * LOG2E
                m = jnp.maximum(m, sink)
                sinks.append(sink)
            ms.append(m)
        for r in range(ng * tq // strip):
            rows = slice(r * strip, (r + 1) * strip)
            m = ms[r]
            lsum = jnp.zeros((strip, LANE), f32)
            for j in range(nblk):
                p = jnp.exp2(s_ref[rows, j * LANE:(j + 1) * LANE] - m)
                lsum = lsum + p
                p_ref[rows, j * LANE:(j + 1) * LANE] = p.astype(bf16)
            l = jnp.broadcast_to(jnp.sum(lsum, axis=-1, keepdims=True), (strip, LANE))
            if use_sink:
                l = l + jnp.exp2(sinks[r] - m)
            ls.append(l)
        vt = v16_ref[pl.ds(pl.multiple_of(w0 * tq, tq), n * tq), :]
        acc = _dot(p_ref[:, 0:n * tq], vt)
        l = jnp.concatenate(ls, axis=0)
        m = jnp.concatenate(ms, axis=0)
        halves = lambda a, tt: jnp.where(lo, a[2 * tt * tq:(2 * tt + 1) * tq], a[(2 * tt + 1) * tq:(2 * tt + 2) * tq])
        for tt in range(nt):
            cols = slice(tt * LANE, (tt + 1) * LANE)
            acc_t = halves(acc, tt)
            m_t = halves(m, tt)
            l_t = halves(l, tt)
            if merge_in:
                m_in = m_in_ref[:, cols]
                m_tot = jnp.maximum(m_t, m_in)
                a, a_in = jnp.exp2(m_t - m_tot), jnp.exp2(m_in - m_tot)
                acc_t = acc_t * a + acc_in_ref[:, cols] * a_in
                l_t = l_t * a + l_in_ref[:, cols] * a_in
                m_t = m_tot
            if stats_out:
                o_ref[:, cols] = acc_t
                m_out_ref[:, cols] = m_t
                l_out_ref[:, cols] = l_t
            else:
                o_ref[:, cols] = acc_t / l_t

    prev = 0
    for n in variants:
        cond = (qi >= prev) if n == variants[-1] else ((qi >= prev) & (qi < n))
        pl.when(cond)(functools.partial(attend, n))
        prev = n


def _band_attn(proj, slopes, sinks, *, patterns, tq, stride=1, partial=None, stats_out=False, mixer_c=False, name):
    bsz, t, ncols = proj.shape
    length = t // stride
    if mixer_c:
        nhp, nt = 1, 4
        head_a, head_t, head_b = 0, 1, 4
        qcol, kcol, vcol = COL_QC // 4, COL_KC, COL_VC
        hp_k, hd = 0, HD_C
    else:
        nhp, nt = 4, 1
        head_a, head_t, head_b = 2, 0, 1
        qcol, kcol, vcol = COL_QA, COL_KA, COL_VA
        hp_k, hd = 1, HD_A
    span = max(w for w, _ in patterns) // stride
    nd = min(-(-span // tq) + 1, length // tq)
    variants = tuple(sorted({-(-nd // 3), -(-2 * nd // 3), nd}))
    kern = functools.partial(_band_attn_kernel, tq=tq, nd=nd, nt=nt, head_a=head_a, head_t=head_t, head_b=head_b,
                             patterns=patterns, stride=stride, use_sink=mixer_c, scale=hd ** -0.5,
                             variants=variants, merge_in=partial is not None, stats_out=stats_out)
    nblk_in, nblk_out = ncols // LANE // nt, 4 // nt
    col_in = lambda br, c: (br % stride) * nblk_in + c
    col_out = lambda br, hp: (br % stride) * nblk_out + hp
    proj_v = proj.reshape(bsz, length, stride * ncols)
    smem = pl.BlockSpec(memory_space=pltpu.SMEM)
    in_specs = [smem, smem,
                pl.BlockSpec((None, tq, nt * LANE), lambda hp, br, qi: (br // stride, qi, col_in(br, qcol + hp))),
                pl.BlockSpec((None, length, LANE), lambda hp, br, qi: (br // stride, 0, col_in(br, kcol + hp * hp_k) if nt == 1
                                                                   else kcol)),
                pl.BlockSpec((None, length, LANE), lambda hp, br, qi: (br // stride, 0, col_in(br, vcol + hp * hp_k) if nt == 1
                                                                   else vcol))]
    operands = [slopes, sinks, proj_v, proj_v, proj_v]
    tile = pl.BlockSpec((None, tq, nt * LANE), lambda hp, br, qi: (br // stride, qi, col_out(br, hp)))
    if partial is not None:
        assert stride == 1
        in_specs += [tile] * 3
        operands += list(partial)
    out_sds = jax.ShapeDtypeStruct((bsz, length, stride * 4 * LANE), f32)
    n_out = 3 if stats_out else 1
    outs = pl.pallas_call(
        kern,
        grid=(nhp, bsz * stride, length // tq),
        in_specs=in_specs,
        out_specs=[tile] * n_out,
        out_shape=[out_sds] * n_out,
        scratch_shapes=[pltpu.VMEM((nd + 1, 2 * nt * tq, tq), f32), pltpu.VMEM((2 * nt * tq, nd * tq), f32),
                        pltpu.VMEM((2 * nt * tq, nd * tq), bf16), pltpu.VMEM((length, LANE), bf16),
                        pltpu.VMEM((length, LANE), bf16)],
        compiler_params=_cparams(3),
        name=name,
    )(*operands)
    outs = [o.reshape(bsz, t, 4 * LANE) for o in outs]
    return outs if stats_out else outs[0]


def _attn_a_sample_kernel(bias_ref, q_ref, kn_ref, vn_ref, kt_ref, vt_ref, o_ref, *, scale):
    nh, hd, p = kt_ref.shape
    for h in range(nh):
        qc = q_ref[h] * scale
        s_new = jnp.sum(qc * kn_ref[h], axis=0, keepdims=True)[:, 0:1]
        s = jnp.concatenate(
            [jnp.sum(kt_ref[h, :, j * LANE:(j + 1) * LANE] * qc, axis=0, keepdims=True) for j in range(p // LANE)],
            axis=1) + bias_ref[h:h + 1, :]
        m = jnp.maximum(jnp.max(s, axis=-1, keepdims=True), s_new)
        pr = jnp.exp(s - m)
        p_new = float(len(A_PATTERNS)) * jnp.exp(s_new - m)
        l = jnp.sum(pr, axis=-1, keepdims=True) + p_new
        acc = vt_ref[h, :, 0:LANE] * pr[:, 0:LANE]
        for j in range(1, p // LANE):
            acc = acc + vt_ref[h, :, j * LANE:(j + 1) * LANE] * pr[:, j * LANE:(j + 1) * LANE]
        o = (jnp.sum(acc, axis=-1, keepdims=True) + p_new * vn_ref[h]) / l
        o_ref[h] = o


def _attn_a_sample(layer, q, kn, vn, cache_kt, cache_vt, slopes):
    _, s, h, hd, p = cache_kt.shape
    dist = (p - np.arange(p)).astype(np.int64)
    mult = sum(((dist <= wd) & (dist % dl == 0)).astype(np.float64) for wd, dl in A_PATTERNS)
    slopes = np.asarray(slopes, np.float32)
    bias = np.where(mult[None] > 0, np.log(np.maximum(mult, 1.0))[None].astype(np.float32)
                    - slopes[:, None] * dist[None].astype(np.float32), np.float32(NEG)).astype(np.float32)
    tok = pl.BlockSpec((None, h, hd, LANE), lambda i: (i, 0, 0, 0))
    cache = pl.BlockSpec((None, None, h, hd, p), lambda i: (layer, i, 0, 0, 0))
    return pl.pallas_call(
        functools.partial(_attn_a_sample_kernel, scale=hd ** -0.5),
        grid=(s,),
        in_specs=[pl.BlockSpec(bias.shape, lambda i: (0, 0)), tok, tok, tok, cache, cache],
        out_specs=tok,
        out_shape=jax.ShapeDtypeStruct((s, h, hd, LANE), f32),
        compiler_params=_cparams(1),
        name="attn_a_sample",
    )(jnp.asarray(bias), q, kn, vn, cache_kt, cache_vt)


def _attn_c_sample_kernel(slope_ref, sink_ref, q_ref, kn_ref, vn_ref, k_ref, v_ref, o_ref, *, bs, scale):
    rows = k_ref.shape[1]
    prow = lax.broadcasted_iota(jnp.int32, (H_C, LANE), 0)
    plane = lax.broadcasted_iota(jnp.int32, (H_C, LANE), 1)
    halfmask = (plane // HD_C) == (prow % 2)
    j = lax.broadcasted_iota(jnp.int32, (H_C, rows), 1)
    dist = rows - j
    slope = slope_ref[...]
    sink = sink_ref[...]
    bias = jnp.where(dist < WIN_C, -slope * dist.astype(f32), NEG)
    for b in range(bs):
        tiles = []
        for t in range(WC_Q // LANE):
            qt = q_ref[b:b + 1, t * LANE:(t + 1) * LANE] * scale
            tiles += [qt, qt]
        qblk = jnp.where(halfmask, jnp.concatenate(tiles, axis=0), 0.0)
        kn = kn_ref[b:b + 1, :]
        vn = vn_ref[b:b + 1, :]
        s_new = jnp.sum(qblk * kn, axis=-1, keepdims=True)
        s = _dot_nt(qblk.astype(bf16), k_ref[b].astype(bf16)) + bias
        m = jnp.maximum(jnp.maximum(s_new, jnp.max(s, axis=-1, keepdims=True)), sink)
        p = jnp.exp(s - m)
        p_new = jnp.exp(s_new - m)
        l = jnp.sum(p, axis=-1, keepdims=True) + p_new + jnp.exp(sink - m)
        acc = _dot(p.astype(bf16), v_ref[b].astype(bf16)) + p_new * vn
        o = jnp.where(halfmask, acc / l, 0.0)
        for t in range(WC_Q // LANE):
            o_ref[b:b + 1, t * LANE:(t + 1) * LANE] = o[2 * t:2 * t + 1, :] + o[2 * t + 1:2 * t + 2, :]


def _attn_c_sample(proj_s, cache_k, cache_v, slopes, sinks, bs):
    s, p, w = cache_k.shape
    assert p == WIN_C and w == LANE
    col = pl.BlockSpec((H_C, 1), lambda i: (0, 0))
    cache = pl.BlockSpec((bs, p, w), lambda i: (i, 0, 0))
    return pl.pallas_call(
        functools.partial(_attn_c_sample_kernel, bs=bs, scale=HD_C ** -0.5),
        grid=(s // bs,),
        in_specs=[col, col,
                  pl.BlockSpec((bs, WC_Q), lambda i: (i, COL_QC * LANE // WC_Q)),
                  pl.BlockSpec((bs, LANE), lambda i: (i, COL_KC)),
                  pl.BlockSpec((bs, LANE), lambda i: (i, COL_VC)),
                  cache, cache],
        out_specs=pl.BlockSpec((bs, WC_Q), lambda i: (i, 0)),
        out_shape=jax.ShapeDtypeStruct((s, WC_Q), f32),
        compiler_params=_cparams(1),
        name="attn_c_sample",
    )(slopes, sinks, proj_s, proj_s, proj_s, cache_k, cache_v)


def _gdn_gates(ab, gp_ref):
    g = -jnp.exp(gp_ref[0:1, :]) * _softplus(ab + gp_ref[1:2, :])
    return g, _sigmoid(ab)


def _l2norm(x):
    return x * lax.rsqrt(jnp.sum(x * x, axis=-1, keepdims=True) + EPS)


def _gated_out(o, z, gn):
    y = o * lax.rsqrt(jnp.mean(o * o, axis=-1, keepdims=True) + EPS) * gn
    return y * _silu(z)


def _gdn_prompt_kernel(x_ref, z_ref, ab_ref, wconv_ref, gp_ref, gn_ref, o_ref, s_ref, buf_ref, act_ref, *, nb, c):
    ti = pl.program_id(0)
    nch = nb * H_B

    @pl.when(ti == 0)
    def _():
        buf_ref[:, 0:8, :] = jnp.zeros((nb, 8, CB_CH), f32)
        s_ref[...] = jnp.zeros(s_ref.shape, f32)

    for b in range(nb):
        buf_ref[b, 8:8 + c, :] = x_ref[b]
        for cb in range(CB_CH // LANE):
            sl = slice(cb * LANE, (cb + 1) * LANE)
            y = buf_ref[b, pl.ds(8 - (CONV_B - 1), c), sl] * wconv_ref[0:1, sl]
            for jw in range(1, CONV_B):
                y = y + buf_ref[b, pl.ds(8 - (CONV_B - 1) + jw, c), sl] * wconv_ref[jw:jw + 1, sl]
            act_ref[b, :, sl] = _silu(y)
        buf_ref[b, 0:8, :] = buf_ref[b, c:c + 8, :]

    g, beta = _gdn_gates(ab_ref[...].reshape(nb * c, LANE), gp_ref)
    rowc = lax.broadcasted_iota(jnp.int32, (nb * c, LANE), 0) & (c - 1)
    sh = 1
    while sh < c:
        g = g + jnp.where(rowc >= sh, pltpu.roll(g, sh, axis=0), 0.0)
        sh *= 2

    qs, ks, vs, zs, gcols, grows, bcols = [], [], [], [], [], [], []
    for b in range(nb):
        gb = g[b * c:(b + 1) * c]
        gb_t = jnp.transpose(gb)
        for h in range(H_B):
            qs.append(_l2norm(act_ref[b, :, h * DK_B:(h + 1) * DK_B]) * (DK_B ** -0.5))
            ks.append(_l2norm(act_ref[b, :, WB_QK + h * DK_B:WB_QK + (h + 1) * DK_B]))
            vs.append(act_ref[b, :, 2 * WB_QK + h * DV_B:2 * WB_QK + (h + 1) * DV_B])
            zs.append(z_ref[b, :, h * DV_B:(h + 1) * DV_B])
            gcols.append(gb[:, h:h + 1])
            grows.append(gb_t[h:h + 1, :])
            bcols.append(beta[b * c:(b + 1) * c, H_B + h:H_B + h + 1])
    q, k, v, z = jnp.stack(qs), jnp.stack(ks), jnp.stack(vs), jnp.stack(zs)
    gcol, grow, bcol = jnp.stack(gcols), jnp.stack(grows), jnp.stack(bcols)

    ri = lax.broadcasted_iota(jnp.int32, (c, c), 0)
    ci = lax.broadcasted_iota(jnp.int32, (c, c), 1)
    incl = (ci <= ri)[None]
    strict = (ci < ri)[None]
    eye = jnp.where(ci == ri, 1.0, 0.0)[None]
    bmm = lambda a, b_: jnp.einsum("nij,njk->nik", a.astype(bf16), b_.astype(bf16), preferred_element_type=f32)
    bmm_nt = lambda a, b_: jnp.einsum("nik,njk->nij", a.astype(bf16), b_.astype(bf16), preferred_element_type=f32)

    decay = jnp.exp(jnp.where(incl, gcol - grow, NEG))
    eg = jnp.exp(gcol)
    kb = k * bcol
    a_low = jnp.where(strict, bmm_nt(kb, k) * decay, 0.0)
    blk = min(16, c)
    rb, cb_ = ri // blk, ci // blk
    pw = jnp.where((rb == cb_)[None], -a_low, 0.0)
    tmat = eye + pw
    for _ in range(int(math.log2(blk)) - 1):
        pw = bmm(pw, pw)
        tmat = tmat + bmm(tmat, pw)
    while blk < c:
        off = ((ri // (2 * blk)) == (ci // (2 * blk))) & (rb != cb_)
        tmat = tmat - bmm(tmat, bmm(jnp.where(off[None], a_low, 0.0), tmat))
        blk *= 2
        rb, cb_ = ri // blk, ci // blk
    u = bmm(tmat, v * bcol)
    w = bmm(tmat, kb * eg)
    qk = jnp.where(incl, bmm_nt(q, k) * decay, 0.0)
    s = s_ref[...].reshape(nch, DK_B, DV_B)
    v_new = u - bmm(w, s)
    o = bmm(q * eg, s) + bmm(qk, v_new)
    glast = gcol[:, c - 1:c, :]
    k_dec = (k * jnp.exp(glast - gcol)).astype(bf16)
    vb = v_new.astype(bf16)
    upd = jnp.stack([_dot_tn(k_dec[n], vb[n]) for n in range(nch)])
    s_ref[...] = (s * jnp.exp(glast) + upd).reshape(nb, H_B, DK_B, DV_B)
    y = _gated_out(o, z, gn_ref[...][None])
    for b in range(nb):
        for h in range(H_B):
            o_ref[b, :, h * DV_B:(h + 1) * DV_B] = y[b * H_B + h]


def _gdn_prompt(proj, w_conv, gate_params, gdn_norm, c):
    bsz, t, _ = proj.shape
    const = lambda shape: pl.BlockSpec(shape, lambda ti: (0,) * len(shape))
    return pl.pallas_call(
        functools.partial(_gdn_prompt_kernel, nb=bsz, c=c),
        grid=(t // c,),
        in_specs=[pl.BlockSpec((bsz, c, CB_CH), lambda ti: (0, ti, COL_CB * LANE // CB_CH)),
                  pl.BlockSpec((bsz, c, WB_V), lambda ti: (0, ti, COL_ZB * LANE // WB_V)),
                  pl.BlockSpec((bsz, c, LANE), lambda ti: (0, ti, COL_AB)),
                  const((CONV_B, CB_CH)), const((8, LANE)), const((1, DV_B))],
        out_specs=[pl.BlockSpec((bsz, c, WB_V), lambda ti: (0, ti, 0)),
                   pl.BlockSpec((bsz, H_B, DK_B, DV_B), lambda ti: (0, 0, 0, 0))],
        out_shape=[jax.ShapeDtypeStruct((bsz, t, WB_V), f32),
                   jax.ShapeDtypeStruct((bsz, H_B, DK_B, DV_B), f32)],
        scratch_shapes=[pltpu.VMEM((bsz, c + 8, CB_CH), f32), pltpu.VMEM((bsz, c, CB_CH), f32)],
        compiler_params=_cparams(1),
        name="gdn_prompt",
    )(proj, proj, proj, w_conv, gate_params, gdn_norm)


def _gdn_sample_kernel(x_ref, c0_ref, c1_ref, c2_ref, z_ref, ab_ref, s_ref, wconv_ref, gp_ref, gn_ref,
                       o_ref, so_ref, *, bs):
    y = (c0_ref[...] * wconv_ref[0:1, :] + c1_ref[...] * wconv_ref[1:2, :] + c2_ref[...] * wconv_ref[2:3, :]
         + x_ref[...] * wconv_ref[3:4, :])
    act = _silu(y)
    g, beta = _gdn_gates(ab_ref[...], gp_ref)
    eg = jnp.exp(g)
    gn = gn_ref[...]
    rowi = lax.broadcasted_iota(jnp.int32, (bs, DK_B), 0)
    for h in range(H_B):
        q = _l2norm(act[:, h * DK_B:(h + 1) * DK_B]) * (DK_B ** -0.5)
        k = _l2norm(act[:, WB_QK + h * DK_B:WB_QK + (h + 1) * DK_B])
        v = act[:, 2 * WB_QK + h * DV_B:2 * WB_QK + (h + 1) * DV_B]
        egc = eg[:, h:h + 1]
        bc = beta[:, H_B + h:H_B + h + 1]
        qg = q * egc
        ks_rows, qs_rows = [], []
        for b in range(bs):
            lhs = jnp.where(rowi == 0, jnp.broadcast_to(k[b:b + 1], (bs, DK_B)),
                            jnp.where(rowi == 1, jnp.broadcast_to(qg[b:b + 1], (bs, DK_B)), 0.0))
            r = _dot(lhs.astype(bf16), s_ref[b, h].astype(bf16))
            ks_rows.append(r[0:1])
            qs_rows.append(r[1:2])
        ks = jnp.concatenate(ks_rows, axis=0)
        qs = jnp.concatenate(qs_rows, axis=0)
        v_new = bc * v - (bc * egc) * ks
        qk = jnp.sum(q * k, axis=-1, keepdims=True)
        o = qs + qk * v_new
        o_ref[:, h * DV_B:(h + 1) * DV_B] = _gated_out(o, z_ref[:, h * DV_B:(h + 1) * DV_B], gn)
        vb = v_new.astype(bf16)
        for b in range(bs):
            km = jnp.where(rowi == b, k, 0.0).astype(bf16)
            so_ref[b, h] = s_ref[b, h] * egc[b:b + 1, :] + _dot_tn(km, vb)


def _gdn_sample(layer, proj_s, conv_state, ssm_state, w_conv, gate_params, gdn_norm, bs):
    s = proj_s.shape[0]
    const = lambda shape: pl.BlockSpec(shape, lambda i: (0,) * len(shape))
    cst = lambda j: pl.BlockSpec((bs, CB_CH), lambda i: (i, j))
    st = pl.BlockSpec((bs, H_B, DK_B, DV_B), lambda i: (i, 0, 0, 0))
    st_in = pl.BlockSpec((None, bs, H_B, DK_B, DV_B), lambda i: (layer, i, 0, 0, 0))
    return pl.pallas_call(
        functools.partial(_gdn_sample_kernel, bs=bs),
        grid=(s // bs,),
        in_specs=[pl.BlockSpec((bs, CB_CH), lambda i: (i, COL_CB * LANE // CB_CH)),
                  cst(0), cst(1), cst(2),
                  pl.BlockSpec((bs, WB_V), lambda i: (i, COL_ZB * LANE // WB_V)),
                  pl.BlockSpec((bs, LANE), lambda i: (i, COL_AB)),
                  st_in, const((CONV_B, CB_CH)), const((8, LANE)), const((1, DV_B))],
        out_specs=[pl.BlockSpec((bs, WB_V), lambda i: (i, 0)), st],
        out_shape=[jax.ShapeDtypeStruct((s, WB_V), f32),
                   jax.ShapeDtypeStruct(ssm_state.shape[1:], f32)],
        compiler_params=_cparams(1),
        name="gdn_sample",
    )(proj_s, conv_state, conv_state, conv_state, proj_s, proj_s, ssm_state, w_conv, gate_params, gdn_norm)


def _merge_kernel(x_ref, g_ref, oa_ref, ob_ref, oc_ref, wa_ref, wb_ref, wc_ref, wo_ref, y_ref):
    d = D_MODEL
    m = _sigmoid(g_ref[:, 0:d]) * _dot(oa_ref[...].astype(bf16), wa_ref[...])
    m = m + _sigmoid(g_ref[:, d:2 * d]) * _dot(ob_ref[...].astype(bf16), wb_ref[...])
    m = m + _sigmoid(g_ref[:, 2 * d:3 * d]) * _dot(oc_ref[...].astype(bf16), wc_ref[...])
    y_ref[...] = x_ref[...] + _dot(m.astype(bf16), wo_ref[...])


def _merge(x, proj, o_a, o_b, o_c, wa, wb, wc, wo, tm):
    m, d = x.shape
    row = lambda w: pl.BlockSpec((tm, w), lambda i: (i, 0))
    const = lambda a: pl.BlockSpec(a.shape, lambda i: (0, 0))
    return pl.pallas_call(
        _merge_kernel,
        grid=(m // tm,),
        in_specs=[row(d), row(3 * d), row(WA), row(WB_V), row(WC_Q), const(wa), const(wb), const(wc), const(wo)],
        out_specs=row(d),
        out_shape=jax.ShapeDtypeStruct((m, d), f32),
        compiler_params=_cparams(1),
        name="merge_out_proj",
    )(x, proj, o_a, o_b, o_c, wa, wb, wc, wo)


def _ffn_prompt_kernel(x_ref, g_ref, wg_ref, wu_ref, cg_ref, cu_ref, bg_ref, bu_ref, wd_ref,
                       y_ref, sg_ref, su_ref, h_ref, bufg_ref, bufu_ref, carg_ref, caru_ref, *, tm):
    ti = pl.program_id(1)
    j = pl.program_id(2)
    kw = CONV_FFN

    @pl.when(j == 0)
    def _():
        x = x_ref[...]
        ms = jnp.mean(x * x, axis=-1, keepdims=True)
        h_ref[...] = (x * lax.rsqrt(ms + EPS) * g_ref[...]).astype(bf16)
        y_ref[...] = x

    @pl.when(ti == 0)
    def _():
        carg_ref[j] = jnp.zeros(carg_ref.shape[1:], f32)
        caru_ref[j] = jnp.zeros(caru_ref.shape[1:], f32)

    def conv(w_ref, cw_ref, b_ref, buf_ref, car_ref, st_ref):
        buf_ref[0:8, :] = car_ref[j]
        buf_ref[8:8 + tm, :] = _dot(h_ref[...], w_ref[...])
        y = b_ref[...] + buf_ref[pl.ds(8 - (kw - 1), tm), :] * cw_ref[0:1, :]
        for jw in range(1, kw):
            y = y + buf_ref[pl.ds(8 - (kw - 1) + jw, tm), :] * cw_ref[jw:jw + 1, :]
        car_ref[j] = buf_ref[tm:tm + 8, :]
        st_ref[...] = buf_ref[tm + 8 - (kw - 1):tm + 8, :]
        return y

    gate = conv(wg_ref, cg_ref, bg_ref, bufg_ref, carg_ref, sg_ref)
    up = conv(wu_ref, cu_ref, bu_ref, bufu_ref, caru_ref, su_ref)
    act = (_gelu_tanh(gate) * up).astype(bf16)
    y_ref[...] += _dot(act, wd_ref[...])


def _ffn_prompt(x, g, w_up, w_conv, b_conv, w_down, tm, tf):
    bsz, t, d = x.shape
    nj = D_FF // tf
    kw = CONV_FFN
    gcol = lambda shape: pl.BlockSpec(shape, lambda b, ti, j: (0, j))
    ucol = lambda shape: pl.BlockSpec(shape, lambda b, ti, j: (0, nj + j))
    y, sg, su = pl.pallas_call(
        functools.partial(_ffn_prompt_kernel, tm=tm),
        grid=(bsz, t // tm, nj),
        in_specs=[pl.BlockSpec((None, tm, d), lambda b, ti, j: (b, ti, 0)),
                  pl.BlockSpec((1, d), lambda b, ti, j: (0, 0)),
                  gcol((d, tf)), ucol((d, tf)), gcol((kw, tf)), ucol((kw, tf)), gcol((1, tf)), ucol((1, tf)),
                  pl.BlockSpec((tf, d), lambda b, ti, j: (j, 0))],
        out_specs=[pl.BlockSpec((None, tm, d), lambda b, ti, j: (b, ti, 0)),
                   pl.BlockSpec((None, None, kw - 1, tf), lambda b, ti, j: (b, ti, 0, j)),
                   pl.BlockSpec((None, None, kw - 1, tf), lambda b, ti, j: (b, ti, 0, j))],
        out_shape=[jax.ShapeDtypeStruct((bsz, t, d), f32),
                   jax.ShapeDtypeStruct((bsz, t // tm, kw - 1, D_FF), f32),
                   jax.ShapeDtypeStruct((bsz, t // tm, kw - 1, D_FF), f32)],
        scratch_shapes=[pltpu.VMEM((tm, d), bf16),
                        pltpu.VMEM((tm + 8, tf), f32), pltpu.VMEM((tm + 8, tf), f32),
                        pltpu.VMEM((nj, 8, tf), f32), pltpu.VMEM((nj, 8, tf), f32)],
        compiler_params=_cparams(3),
        name="conv_ffn_prompt",
    )(x, g, w_up, w_up, w_conv, w_conv, b_conv, b_conv, w_down)
    return y, jnp.concatenate([sg[:, -1], su[:, -1]], axis=-1)


def _ffn_sample_kernel(x_ref, g_ref, wg_ref, wu_ref, cg_ref, cu_ref, bg_ref, bu_ref, wd_ref,
                       pg0_ref, pg1_ref, pu0_ref, pu1_ref, y_ref, ug_ref, uu_ref, h_ref):
    @pl.when(pl.program_id(0) == 0)
    def _():
        x = x_ref[...]
        ms = jnp.mean(x * x, axis=-1, keepdims=True)
        h_ref[...] = (x * lax.rsqrt(ms + EPS) * g_ref[...]).astype(bf16)
        y_ref[...] = x

    def conv(w_ref, cw_ref, b_ref, p0_ref, p1_ref, u_ref):
        u = _dot(h_ref[...], w_ref[...])
        u_ref[...] = u
        return b_ref[...] + p0_ref[...] * cw_ref[0:1, :] + p1_ref[...] * cw_ref[1:2, :] + u * cw_ref[2:3, :]

    gate = conv(wg_ref, cg_ref, bg_ref, pg0_ref, pg1_ref, ug_ref)
    up = conv(wu_ref, cu_ref, bu_ref, pu0_ref, pu1_ref, uu_ref)
    y_ref[...] += _dot((_gelu_tanh(gate) * up).astype(bf16), wd_ref[...])


def _ffn_sample(x, g, w_up, w_conv, b_conv, w_down, state, tf):
    s, d = x.shape
    nj = D_FF // tf
    kw = CONV_FFN
    assert kw == 3
    col = lambda shape, off: pl.BlockSpec(shape, lambda j: (0, off + j))
    y, ug, uu = pl.pallas_call(
        _ffn_sample_kernel,
        grid=(nj,),
        in_specs=[pl.BlockSpec((s, d), lambda j: (0, 0)), pl.BlockSpec((1, d), lambda j: (0, 0)),
                  col((d, tf), 0), col((d, tf), nj), col((kw, tf), 0), col((kw, tf), nj),
                  col((1, tf), 0), col((1, tf), nj),
                  pl.BlockSpec((tf, d), lambda j: (j, 0)),
                  col((s, tf), 0), col((s, tf), 2 * nj), col((s, tf), nj), col((s, tf), 3 * nj)],
        out_specs=[pl.BlockSpec((s, d), lambda j: (0, 0)), col((s, tf), 0), col((s, tf), 0)],
        out_shape=[jax.ShapeDtypeStruct((s, d), f32),
                   jax.ShapeDtypeStruct((s, D_FF), f32), jax.ShapeDtypeStruct((s, D_FF), f32)],
        scratch_shapes=[pltpu.VMEM((s, d), bf16)],
        compiler_params=_cparams(1),
        name="conv_ffn_sample",
    )(x, g, w_up, w_up, w_conv, w_conv, b_conv, b_conv, w_down, state, state, state, state)
    return y, jnp.concatenate([ug, uu], axis=-1)


def _final_norm_kernel(x_ref, g_ref, y_ref):
    x = x_ref[...]
    ms = jnp.mean(x * x, axis=-1, keepdims=True)
    y_ref[...] = x * lax.rsqrt(ms + EPS) * g_ref[...]


def _final_norm(x, g, tm):
    m, d = x.shape
    return pl.pallas_call(
        _final_norm_kernel,
        grid=(m // tm,),
        in_specs=[pl.BlockSpec((tm, d), lambda i: (i, 0)), pl.BlockSpec((1, d), lambda i: (0, 0))],
        out_specs=pl.BlockSpec((tm, d), lambda i: (i, 0)),
        out_shape=jax.ShapeDtypeStruct((m, d), f32),
        compiler_params=_cparams(1),
        name="final_norm",
    )(x, g)


def _pack_w_in(w_in):
    splits = np.cumsum([WA, WA, WA, CB_CH, WB_V, H_B, H_B, WC_Q, WC_KV, WC_KV])
    qa, ka, va, cb, zb, adec, bgate, qc, kc, vc, gates = jnp.split(w_in, splits.tolist(), axis=-1)
    lead = w_in.shape[:-1]
    qc = qc.reshape(lead + (H_C, HD_C))[..., list(C_HEAD_ORDER), :].reshape(lead + (WC_Q,))
    ab = jnp.concatenate([adec, bgate, jnp.zeros(lead + (LANE - 2 * H_B,), w_in.dtype)], axis=-1)
    pad = jnp.zeros(lead + (LANE,), w_in.dtype)
    packed = jnp.concatenate([gates, qa, ka, va, cb, zb, qc, ab, kc, vc, pad], axis=-1)
    assert packed.shape[-1] == N_COLS
    return packed.astype(bf16)


def _pick(n, pref):
    for t in pref:
        if n % t == 0:
            return t
    return n


def kernel(x_prompt, x_sample, cache_a_k, cache_a_v, cache_c_k, cache_c_v, state_b_conv, state_b_ssm, state_ffn_conv, norm_mix, w_in, w_conv_b, a_log, dt_bias, gdn_norm, sinks, w_br_a, w_br_b, w_br_c, w_out, norm_ffn, w_up, w_conv_ffn, b_conv_ffn, w_down, norm_final):
    depth = w_in.shape[0]
    bp, t, d = x_prompt.shape
    ns = x_sample.shape[0]
    assert x_sample.shape[1] == 1 and d == D_MODEL and t % 256 == 0

    slopes_a, slopes_c = _alibi_slopes()
    order = list(C_HEAD_ORDER)
    slopes_c_col = jnp.asarray(slopes_c[order]).reshape(H_C, 1)

    w_in_p = _pack_w_in(w_in)
    w_br_a16 = w_br_a.astype(bf16)
    w_br_b16 = w_br_b.astype(bf16)
    w_br_c16 = w_br_c.reshape(depth, H_C, HD_C, d)[:, order].reshape(depth, WC_Q, d).astype(bf16)
    w_out16 = w_out.astype(bf16)
    w_up16 = w_up.astype(bf16)
    w_down16 = w_down.astype(bf16)
    gate_params = jnp.zeros((depth, 8, LANE), f32)
    gate_params = gate_params.at[:, 0, :H_B].set(a_log.astype(f32)).at[:, 1, :H_B].set(dt_bias.astype(f32))

    tm_p = _pick(bp * t, (2048, 1024, 512, 256))
    tm_ffn = _pick(t, (1024, 512, 256))
    tn = _pick(N_COLS, (512,))
    tf = 512
    bs = _pick(ns, (8,))
    cache_a_kt = jnp.transpose(cache_a_k, (0, 1, 3, 4, 2))
    cache_a_vt = jnp.transpose(cache_a_v, (0, 1, 3, 4, 2))

    xp = x_prompt.reshape(bp * t, d)
    xs = x_sample.reshape(ns, d)
    outs = {k: [] for k in ("akp", "avp", "ckp", "cvp", "bcp", "bsp", "fcp", "aks", "avs", "cks", "cvs", "bcs",
                            "bss", "fcs")}
    ra = min(WIN_A, t)
    rc = min(WIN_C, t)
    for l in range(depth):
        g_mix = norm_mix[l].reshape(1, d)
        sinks_l = sinks[l].astype(f32)
        proj = _norm_matmul(xp, g_mix, w_in_p[l], tm_p, tn)
        proj3 = proj.reshape(bp, t, N_COLS)
        far_w, far_d = A_PATTERNS[-1]
        if (t // far_d) % LANE == 0:
            far = _band_attn(proj3, jnp.asarray(slopes_a), sinks_l, patterns=A_PATTERNS[-1:], stride=far_d,
                             tq=min(t // far_d, 256), stats_out=True, name="band_attn_a_far")
            o_a = _band_attn(proj3, jnp.asarray(slopes_a), sinks_l, patterns=A_PATTERNS[:-1], tq=256, partial=far,
                             name="band_attn_a_near")
        else:
            o_a = _band_attn(proj3, jnp.asarray(slopes_a), sinks_l, patterns=A_PATTERNS, tq=256, name="band_attn_a")
        o_c = _band_attn(proj3, jnp.asarray(slopes_c), sinks_l, patterns=((WIN_C - 1, 1),), tq=128, mixer_c=True,
                         name="band_attn_c")
        o_b, ssm_p = _gdn_prompt(proj3, w_conv_b[l], gate_params[l], gdn_norm[l].reshape(1, DV_B), c=128)
        xp = _merge(xp, proj, o_a.reshape(bp * t, WA), o_b.reshape(bp * t, WB_V), o_c.reshape(bp * t, WC_Q),
                    w_br_a16[l], w_br_b16[l], w_br_c16[l], w_out16[l], 256)
        y3, fc_p = _ffn_prompt(xp.reshape(bp, t, d), norm_ffn[l].reshape(1, d), w_up16[l], w_conv_ffn[l],
                               b_conv_ffn[l].reshape(1, 2 * D_FF), w_down16[l], tm_ffn, tf)
        xp = y3.reshape(bp * t, d)
        ka = proj3[:, t - ra:, COL_KA * LANE:COL_KA * LANE + WA]
        va = proj3[:, t - ra:, COL_VA * LANE:COL_VA * LANE + WA]
        outs["akp"].append(ka.reshape(bp, ra, H_A, HD_A))
        outs["avp"].append(va.reshape(bp, ra, H_A, HD_A))
        outs["ckp"].append(proj3[:, t - rc:, COL_KC * LANE:(COL_KC + 1) * LANE].reshape(bp, rc, KVH_C, HD_C))
        outs["cvp"].append(proj3[:, t - rc:, COL_VC * LANE:(COL_VC + 1) * LANE].reshape(bp, rc, KVH_C, HD_C))
        outs["bcp"].append(proj3[:, t - (CONV_B - 1):, COL_CB * LANE:COL_CB * LANE + CB_CH])
        outs["bsp"].append(ssm_p)
        outs["fcp"].append(fc_p)
        proj_s = _norm_matmul(xs, g_mix, w_in_p[l], ns, tn)
        heads_a = lambda col: proj_s[:, col * LANE:col * LANE + WA].reshape(ns, H_A, HD_A)
        ka_s, va_s = heads_a(COL_KA), heads_a(COL_VA)
        lanes = lambda a: jnp.broadcast_to(a[..., None], a.shape + (LANE,))
        o_a = _attn_a_sample(l, lanes(heads_a(COL_QA)), lanes(ka_s), lanes(va_s), cache_a_kt, cache_a_vt,
                             slopes_a)[..., 0].reshape(ns, WA)
        pc = cache_c_k.shape[2]
        o_c = _attn_c_sample(proj_s, cache_c_k[l].reshape(ns, pc, WC_KV), cache_c_v[l].reshape(ns, pc, WC_KV),
                             slopes_c_col, sinks_l[jnp.asarray(order)].reshape(H_C, 1), bs)
        conv_st = state_b_conv[l]
        o_b, ssm_s = _gdn_sample(l, proj_s, conv_st.reshape(ns, (CONV_B - 1) * CB_CH), state_b_ssm, w_conv_b[l],
                                 gate_params[l], gdn_norm[l].reshape(1, DV_B), bs)
        xs = _merge(xs, proj_s, o_a, o_b, o_c, w_br_a16[l], w_br_b16[l], w_br_c16[l], w_out16[l], ns)
        ffn_st = state_ffn_conv[l]
        xs, u_s = _ffn_sample(xs, norm_ffn[l].reshape(1, d), w_up16[l], w_conv_ffn[l],
                              b_conv_ffn[l].reshape(1, 2 * D_FF), w_down16[l],
                              ffn_st.reshape(ns, (CONV_FFN - 1) * 2 * D_FF), tf)
        outs["aks"].append(ka_s[:, None])
        outs["avs"].append(va_s[:, None])
        outs["cks"].append(proj_s[:, COL_KC * LANE:(COL_KC + 1) * LANE].reshape(ns, 1, KVH_C, HD_C))
        outs["cvs"].append(proj_s[:, COL_VC * LANE:(COL_VC + 1) * LANE].reshape(ns, 1, KVH_C, HD_C))
        xb_new = proj_s[:, COL_CB * LANE:COL_CB * LANE + CB_CH]
        outs["bcs"].append(jnp.concatenate([conv_st[:, 1:], xb_new[:, None, :]], axis=1))
        outs["bss"].append(ssm_s)
        outs["fcs"].append(jnp.concatenate([ffn_st[:, 1:], u_s[:, None, :]], axis=1))

    g_fin = norm_final.reshape(1, d)
    y_prompt = _final_norm(xp, g_fin, tm_p).reshape(bp, t, d)
    y_sample = _final_norm(xs, g_fin, ns).reshape(ns, 1, d)
    st = {k: jnp.stack(v) for k, v in outs.items()}
    return (y_prompt, y_sample, st["akp"], st["avp"], st["ckp"], st["cvp"], st["bcp"], st["bsp"], st["fcp"],
            st["aks"], st["avs"], st["cks"], st["cvs"], st["bcs"], st["bss"], st["fcs"])
```

```python
import functools
import math

import numpy as np
import jax
import jax.numpy as jnp
from jax import lax
from jax.experimental import pallas as pl
from jax.experimental.pallas import tpu as pltpu

D_MODEL = 1024
H_A, HD_A = 8, 64
A_PATTERNS = ((128, 1), (512, 4), (2048, 16))
WIN_A = 2048
H_B, DK_B, DV_B = 4, 128, 128
CONV_B = 4
CHUNK_B = 64
H_C, KVH_C, HD_C = 8, 2, 64
WIN_C = 128
D_FF = 3 * D_MODEL
CONV_FFN = 3
EPS = 1e-6
NEG = -1e30
LOG2E = math.log2(math.e)

WA = H_A * HD_A
WB_QK = H_B * DK_B
WB_V = H_B * DV_B
WC_Q = H_C * HD_C
WC_KV = KVH_C * HD_C
CB_CH = 2 * WB_QK + WB_V

LANE = 128
C_HEAD_ORDER = (0, 4, 1, 5, 2, 6, 3, 7)

COL_GATES = 0
COL_QA, COL_KA, COL_VA = 24, 28, 32
COL_CB = 36
COL_ZB = 48
COL_QC = 52
COL_AB = 56
COL_KC, COL_VC = 57, 58
N_COLS = 60 * LANE

VMEM_LIMIT = 48 * 1024 * 1024

f32 = jnp.float32
bf16 = jnp.bfloat16


def _cparams(n_axes):
    return pltpu.CompilerParams(dimension_semantics=("arbitrary",) * n_axes, vmem_limit_bytes=VMEM_LIMIT)


def _alibi_slopes():
    n = H_A + H_C
    s = (2.0 ** (-8.0 * (np.arange(n) + 1) / n)).astype(np.float32)
    return s[H_C:], s[:H_C]


def _sigmoid(x):
    return 1.0 / (1.0 + jnp.exp(-x))


def _silu(x):
    return x * _sigmoid(x)


def _softplus(x):
    return jnp.maximum(x, 0.0) + jnp.log1p(jnp.exp(-jnp.abs(x)))


def _gelu_tanh(x):
    c = math.sqrt(2.0 / math.pi)
    return 0.5 * x * (1.0 + jnp.tanh(c * (x + 0.044715 * (x * x * x))))


def _dot(a, b):
    return jnp.dot(a, b, preferred_element_type=f32)


def _dot_nt(a, b):
    return lax.dot_general(a, b, (((1,), (1,)), ((), ())), preferred_element_type=f32)


def _dot_tn(a, b):
    return lax.dot_general(a, b, (((0,), (0,)), ((), ())), preferred_element_type=f32)


def _dot_f32(a, b):
    return jnp.dot(a, b, preferred_element_type=f32, precision=lax.Precision.HIGHEST)


def _norm_matmul_kernel(x_ref, g_ref, w_ref, o_ref, h_ref):
    @pl.when(pl.program_id(1) == 0)
    def _():
        x = x_ref[...]
        ms = jnp.mean(x * x, axis=-1, keepdims=True)
        h_ref[...] = (x * lax.rsqrt(ms + EPS) * g_ref[...]).astype(bf16)

    o_ref[...] = _dot(h_ref[...], w_ref[...])


def _norm_matmul(x, g, w, tm, tn):
    m, d = x.shape
    n = w.shape[1]
    return pl.pallas_call(
        _norm_matmul_kernel,
        grid=(m // tm, n // tn),
        in_specs=[pl.BlockSpec((tm, d), lambda i, j: (i, 0)),
                  pl.BlockSpec((1, d), lambda i, j: (0, 0)),
                  pl.BlockSpec((d, tn), lambda i, j: (0, j))],
        out_specs=pl.BlockSpec((tm, tn), lambda i, j: (i, j)),
        out_shape=jax.ShapeDtypeStruct((m, n), f32),
        scratch_shapes=[pltpu.VMEM((tm, d), bf16)],
        compiler_params=_cparams(2),
        name="norm_in_proj",
    )(x, g, w)


def _band_bias(patterns, stride, dist, slope):
    tok = dist * stride
    mult = jnp.zeros(dist.shape, f32)
    for w, d in patterns:
        hit = (tok <= w) & ((tok & (d - 1)) == 0)
        mult = mult + jnp.where(hit, 1.0, 0.0)
    logm = jnp.where(mult > 2.5, math.log(3.0), jnp.where(mult > 1.5, math.log(2.0), 0.0))
    return jnp.where((dist >= 0) & (mult > 0.5), logm - slope * tok.astype(f32), NEG)


def _band_attn_kernel(slope_ref, sink_ref, q_ref, k_ref, v_ref, *rest,
                      tq, nd, nt, head_a, head_t, head_b, patterns, stride, use_sink, scale, variants,
                      merge_in, stats_out):
    if merge_in:
        acc_in_ref, m_in_ref, l_in_ref, *rest = rest
    if stats_out:
        o_ref, m_out_ref, l_out_ref, *rest = rest
    else:
        o_ref, *rest = rest
    if stride == 1:
        bias_ref, s_ref, p_ref, k16_ref, v16_ref = rest
    else:
        bias_ref, s_ref, p_ref = rest
    hp = pl.program_id(0)
    b = pl.program_id(1)
    qi = pl.program_id(2)
    ng = 2 * nt
    strip = 64
    head = lambda tt, hh: hp * head_a + tt * head_t + hh * head_b

    @pl.when((b == 0) & (qi == 0))
    def _():
        row = lax.broadcasted_iota(jnp.int32, (tq, tq), 0)
        col = lax.broadcasted_iota(jnp.int32, (tq, tq), 1)
        for tt in range(nt):
            for hh in range(2):
                rows = slice((2 * tt + hh) * tq, (2 * tt + hh + 1) * tq)
                slope = slope_ref[head(tt, hh)]
                for d in range(nd):
                    bias = _band_bias(patterns, stride, d * tq + row - col, slope)
                    bias_ref[d, rows, :] = jnp.where(bias > 0.5 * NEG, bias * LOG2E, NEG)
        bias_ref[nd] = jnp.full((ng * tq, tq), NEG, f32)

    if stride == 1:
        @pl.when(qi == 0)
        def _():
            k16_ref[...] = k_ref[...].astype(bf16)
            v16_ref[...] = v_ref[...].astype(bf16)

    lane = lax.broadcasted_iota(jnp.int32, (tq, LANE), 1)
    lo = lane < LANE // 2
    rows_of = lambda cls: slice(None) if stride == 1 else pl.ds(cls, tq, stride=stride)

    def attend(n, cls=0):
        blocks = []
        for tt in range(nt):
            q = q_ref[rows_of(cls), tt * LANE:(tt + 1) * LANE] * (scale * LOG2E)
            blocks += [jnp.where(lo, q, 0.0), jnp.where(lo, 0.0, q)]
        qs = jnp.concatenate(blocks, axis=0).astype(bf16)
        w0 = jnp.maximum(qi - (n - 1), 0)
        for c in range(n):
            delta = qi - (w0 + c)
            bidx = jnp.where((delta >= 0) & (delta < nd), delta, nd)
            if stride == 1:
                kt = k16_ref[pl.ds(pl.multiple_of((w0 + c) * tq, tq), tq), :]
            else:
                kt = k_ref[rows_of(cls), :].astype(bf16)
            s_ref[:, c * tq:(c + 1) * tq] = _dot_nt(qs, kt) + bias_ref[bidx]
        nblk = n * tq // LANE
        ls, ms, sinks = [], [], []
        for r in range(ng * tq // strip):
            rows = slice(r * strip, (r + 1) * strip)
            mx = s_ref[rows, 0:LANE]
            for j in range(1, nblk):
                mx = jnp.maximum(mx, s_ref[rows, j * LANE:(j + 1) * LANE])
            m = jnp.broadcast_to(jnp.max(mx, axis=-1, keepdims=True), (strip, LANE))
            if use_sink:
                g = r * strip // tq
                sink = jnp.full((strip, LANE), sink_ref[head(g // 2, g % 2)], f32) * LOG2E
                m = jnp.maximum(m, sink)
                sinks.append(sink)
            ms.append(m)
        for r in range(ng * tq // strip):
            rows = slice(r * strip, (r + 1) * strip)
            m = ms[r]
            lsum = jnp.zeros((strip, LANE), f32)
            for j in range(nblk):
                p = jnp.exp2(s_ref[rows, j * LANE:(j + 1) * LANE] - m)
                lsum = lsum + p
                p_ref[rows, j * LANE:(j + 1) * LANE] = p.astype(bf16)
            l = jnp.broadcast_to(jnp.sum(lsum, axis=-1, keepdims=True), (strip, LANE))
            if use_sink:
                l = l + jnp.exp2(sinks[r] - m)
            ls.append(l)
        if stride == 1:
            vt = v16_ref[pl.ds(pl.multiple_of(w0 * tq, tq), n * tq), :]
        else:
            vt = v_ref[rows_of(cls), :].astype(bf16)
        acc = _dot(p_ref[:, 0:n * tq], vt)
        l = jnp.concatenate(ls, axis=0)
        m = jnp.concatenate(ms, axis=0)
        halves = lambda a, tt: jnp.where(lo, a[2 * tt * tq:(2 * tt + 1) * tq], a[(2 * tt + 1) * tq:(2 * tt + 2) * tq])
        for tt in range(nt):
            cols = slice(tt * LANE, (tt + 1) * LANE)
            acc_t = halves(acc, tt)
            m_t = halves(m, tt)
            l_t = halves(l, tt)
            if merge_in:
                m_in = m_in_ref[:, cols]
                m_tot = jnp.maximum(m_t, m_in)
                a, a_in = jnp.exp2(m_t - m_tot), jnp.exp2(m_in - m_tot)
                acc_t = acc_t * a + acc_in_ref[:, cols] * a_in
                l_t = l_t * a + l_in_ref[:, cols] * a_in
                m_t = m_tot
            if stats_out:
                o_ref[rows_of(cls), cols] = acc_t
                m_out_ref[rows_of(cls), cols] = m_t
                l_out_ref[rows_of(cls), cols] = l_t
            else:
                o_ref[rows_of(cls), cols] = acc_t / l_t

    if stride > 1:
        for cls in range(stride):
            attend(1, cls)
        return
    prev = 0
    for n in variants:
        cond = (qi >= prev) if n == variants[-1] else ((qi >= prev) & (qi < n))
        pl.when(cond)(functools.partial(attend, n))
        prev = n


def _band_attn(proj, slopes, sinks, *, patterns, tq, stride=1, partial=None, stats_out=False, mixer_c=False, name):
    bsz, t, ncols = proj.shape
    length = t // stride
    assert stride == 1 or (length == tq and partial is None)
    if mixer_c:
        nhp, nt = 1, 4
        head_a, head_t, head_b = 0, 1, 4
        qcol, kcol, vcol = COL_QC // 4, COL_KC, COL_VC
        hp_k, hd = 0, HD_C
    else:
        nhp, nt = 4, 1
        head_a, head_t, head_b = 2, 0, 1
        qcol, kcol, vcol = COL_QA, COL_KA, COL_VA
        hp_k, hd = 1, HD_A
    span = max(w for w, _ in patterns) // stride
    nd = min(-(-span // tq) + 1, length // tq)
    variants = tuple(sorted({-(-nd // 3), -(-2 * nd // 3), nd}))
    kern = functools.partial(_band_attn_kernel, tq=tq, nd=nd, nt=nt, head_a=head_a, head_t=head_t, head_b=head_b,
                             patterns=patterns, stride=stride, use_sink=mixer_c, scale=hd ** -0.5,
                             variants=variants, merge_in=partial is not None, stats_out=stats_out)
    rows = tq * stride
    smem = pl.BlockSpec(memory_space=pltpu.SMEM)
    in_specs = [smem, smem,
                pl.BlockSpec((None, rows, nt * LANE), lambda hp, b, qi: (b, qi, qcol + hp)),
                pl.BlockSpec((None, t, LANE), lambda hp, b, qi: (b, 0, kcol + hp * hp_k)),
                pl.BlockSpec((None, t, LANE), lambda hp, b, qi: (b, 0, vcol + hp * hp_k))]
    operands = [slopes, sinks, proj, proj, proj]
    tile = pl.BlockSpec((None, rows, nt * LANE), lambda hp, b, qi: (b, qi, hp))
    if partial is not None:
        in_specs += [tile] * 3
        operands += list(partial)
    scratch = [pltpu.VMEM((nd + 1, 2 * nt * tq, tq), f32), pltpu.VMEM((2 * nt * tq, nd * tq), f32),
               pltpu.VMEM((2 * nt * tq, nd * tq), bf16)]
    if stride == 1:
        scratch += [pltpu.VMEM((t, LANE), bf16), pltpu.VMEM((t, LANE), bf16)]
    n_out = 3 if stats_out else 1
    outs = pl.pallas_call(
        kern,
        grid=(nhp, bsz, t // rows),
        in_specs=in_specs,
        out_specs=[tile] * n_out,
        out_shape=[jax.ShapeDtypeStruct((bsz, t, 4 * LANE), f32)] * n_out,
        scratch_shapes=scratch,
        compiler_params=_cparams(3),
        name=name,
    )(*operands)
    return outs if stats_out else outs[0]


def _attn_a_sample_kernel(bias_ref, q_ref, kn_ref, vn_ref, kt_ref, vt_ref, o_ref, *, scale):
    nh, hd, p = kt_ref.shape
    for h in range(nh):
        qc = q_ref[h] * scale
        s_new = jnp.sum(qc * kn_ref[h], axis=0, keepdims=True)[:, 0:1]
        s = jnp.concatenate(
            [jnp.sum(kt_ref[h, :, j * LANE:(j + 1) * LANE] * qc, axis=0, keepdims=True) for j in range(p // LANE)],
            axis=1) + bias_ref[h:h + 1, :]
        m = jnp.maximum(jnp.max(s, axis=-1, keepdims=True), s_new)
        pr = jnp.exp(s - m)
        p_new = float(len(A_PATTERNS)) * jnp.exp(s_new - m)
        l = jnp.sum(pr, axis=-1, keepdims=True) + p_new
        acc = vt_ref[h, :, 0:LANE] * pr[:, 0:LANE]
        for j in range(1, p // LANE):
            acc = acc + vt_ref[h, :, j * LANE:(j + 1) * LANE] * pr[:, j * LANE:(j + 1) * LANE]
        o = (jnp.sum(acc, axis=-1, keepdims=True) + p_new * vn_ref[h]) / l
        o_ref[h] = o


def _attn_a_sample(layer, q, kn, vn, cache_kt, cache_vt, slopes):
    _, s, h, hd, p = cache_kt.shape
    dist = (p - np.arange(p)).astype(np.int64)
    mult = sum(((dist <= wd) & (dist % dl == 0)).astype(np.float64) for wd, dl in A_PATTERNS)
    slopes = np.asarray(slopes, np.float32)
    bias = np.where(mult[None] > 0, np.log(np.maximum(mult, 1.0))[None].astype(np.float32)
                    - slopes[:, None] * dist[None].astype(np.float32), np.float32(NEG)).astype(np.float32)
    tok = pl.BlockSpec((None, h, hd, LANE), lambda i: (i, 0, 0, 0))
    cache = pl.BlockSpec((None, None, h, hd, p), lambda i: (layer, i, 0, 0, 0))
    return pl.pallas_call(
        functools.partial(_attn_a_sample_kernel, scale=hd ** -0.5),
        grid=(s,),
        in_specs=[pl.BlockSpec(bias.shape, lambda i: (0, 0)), tok, tok, tok, cache, cache],
        out_specs=tok,
        out_shape=jax.ShapeDtypeStruct((s, h, hd, LANE), f32),
        compiler_params=_cparams(1),
        name="attn_a_sample",
    )(jnp.asarray(bias), q, kn, vn, cache_kt, cache_vt)


def _attn_c_sample_kernel(slope_ref, sink_ref, q_ref, kn_ref, vn_ref, k_ref, v_ref, o_ref, *, bs, scale):
    rows = k_ref.shape[1]
    prow = lax.broadcasted_iota(jnp.int32, (H_C, LANE), 0)
    plane = lax.broadcasted_iota(jnp.int32, (H_C, LANE), 1)
    halfmask = (plane // HD_C) == (prow % 2)
    j = lax.broadcasted_iota(jnp.int32, (H_C, rows), 1)
    dist = rows - j
    slope = slope_ref[...]
    sink = sink_ref[...]
    bias = jnp.where(dist < WIN_C, -slope * dist.astype(f32), NEG)
    for b in range(bs):
        tiles = []
        for t in range(WC_Q // LANE):
            qt = q_ref[b:b + 1, t * LANE:(t + 1) * LANE] * scale
            tiles += [qt, qt]
        qblk = jnp.where(halfmask, jnp.concatenate(tiles, axis=0), 0.0)
        kn = kn_ref[b:b + 1, :]
        vn = vn_ref[b:b + 1, :]
        s_new = jnp.sum(qblk * kn, axis=-1, keepdims=True)
        s = _dot_nt(qblk.astype(bf16), k_ref[b].astype(bf16)) + bias
        m = jnp.maximum(jnp.maximum(s_new, jnp.max(s, axis=-1, keepdims=True)), sink)
        p = jnp.exp(s - m)
        p_new = jnp.exp(s_new - m)
        l = jnp.sum(p, axis=-1, keepdims=True) + p_new + jnp.exp(sink - m)
        acc = _dot(p.astype(bf16), v_ref[b].astype(bf16)) + p_new * vn
        o = jnp.where(halfmask, acc / l, 0.0)
        for t in range(WC_Q // LANE):
            o_ref[b:b + 1, t * LANE:(t + 1) * LANE] = o[2 * t:2 * t + 1, :] + o[2 * t + 1:2 * t + 2, :]


def _attn_c_sample(proj_s, cache_k, cache_v, slopes, sinks, bs):
    s, p, w = cache_k.shape
    assert p == WIN_C and w == LANE
    col = pl.BlockSpec((H_C, 1), lambda i: (0, 0))
    cache = pl.BlockSpec((bs, p, w), lambda i: (i, 0, 0))
    return pl.pallas_call(
        functools.partial(_attn_c_sample_kernel, bs=bs, scale=HD_C ** -0.5),
        grid=(s // bs,),
        in_specs=[col, col,
                  pl.BlockSpec((bs, WC_Q), lambda i: (i, COL_QC * LANE // WC_Q)),
                  pl.BlockSpec((bs, LANE), lambda i: (i, COL_KC)),
                  pl.BlockSpec((bs, LANE), lambda i: (i, COL_VC)),
                  cache, cache],
        out_specs=pl.BlockSpec((bs, WC_Q), lambda i: (i, 0)),
        out_shape=jax.ShapeDtypeStruct((s, WC_Q), f32),
        compiler_params=_cparams(1),
        name="attn_c_sample",
    )(slopes, sinks, proj_s, proj_s, proj_s, cache_k, cache_v)


def _gdn_gates(ab, gp_ref):
    g = -jnp.exp(gp_ref[0:1, :]) * _softplus(ab + gp_ref[1:2, :])
    return g, _sigmoid(ab)


def _l2norm(x):
    return x * lax.rsqrt(jnp.sum(x * x, axis=-1, keepdims=True) + EPS)


def _gated_out(o, z, gn):
    y = o * lax.rsqrt(jnp.mean(o * o, axis=-1, keepdims=True) + EPS) * gn
    return y * _silu(z)


def _gdn_prompt_kernel(x_ref, z_ref, ab_ref, wconv_ref, gp_ref, gn_ref, o_ref, s_ref, buf_ref, act_ref, *, nb, c):
    ti = pl.program_id(0)
    nch = nb * H_B

    @pl.when(ti == 0)
    def _():
        buf_ref[:, 0:8, :] = jnp.zeros((nb, 8, CB_CH), f32)
        s_ref[...] = jnp.zeros(s_ref.shape, f32)

    for b in range(nb):
        buf_ref[b, 8:8 + c, :] = x_ref[b]
        for cb in range(CB_CH // LANE):
            sl = slice(cb * LANE, (cb + 1) * LANE)
            y = buf_ref[b, pl.ds(8 - (CONV_B - 1), c), sl] * wconv_ref[0:1, sl]
            for jw in range(1, CONV_B):
                y = y + buf_ref[b, pl.ds(8 - (CONV_B - 1) + jw, c), sl] * wconv_ref[jw:jw + 1, sl]
            act_ref[b, :, sl] = _silu(y)
        buf_ref[b, 0:8, :] = buf_ref[b, c:c + 8, :]

    g, beta = _gdn_gates(ab_ref[...].reshape(nb * c, LANE), gp_ref)
    rowc = lax.broadcasted_iota(jnp.int32, (nb * c, LANE), 0) & (c - 1)
    sh = 1
    while sh < c:
        g = g + jnp.where(rowc >= sh, pltpu.roll(g, sh, axis=0), 0.0)
        sh *= 2

    qs, ks, vs, zs, gcols, grows, bcols = [], [], [], [], [], [], []
    for b in range(nb):
        gb = g[b * c:(b + 1) * c]
        gb_t = jnp.transpose(gb)
        for h in range(H_B):
            qs.append(_l2norm(act_ref[b, :, h * DK_B:(h + 1) * DK_B]) * (DK_B ** -0.5))
            ks.append(_l2norm(act_ref[b, :, WB_QK + h * DK_B:WB_QK + (h + 1) * DK_B]))
            vs.append(act_ref[b, :, 2 * WB_QK + h * DV_B:2 * WB_QK + (h + 1) * DV_B])
            zs.append(z_ref[b, :, h * DV_B:(h + 1) * DV_B])
            gcols.append(gb[:, h:h + 1])
            grows.append(gb_t[h:h + 1, :])
            bcols.append(beta[b * c:(b + 1) * c, H_B + h:H_B + h + 1])
    q, k, v, z = jnp.stack(qs), jnp.stack(ks), jnp.stack(vs), jnp.stack(zs)
    gcol, grow, bcol = jnp.stack(gcols), jnp.stack(grows), jnp.stack(bcols)

    ri = lax.broadcasted_iota(jnp.int32, (c, c), 0)
    ci = lax.broadcasted_iota(jnp.int32, (c, c), 1)
    incl = (ci <= ri)[None]
    strict = (ci < ri)[None]
    eye = jnp.where(ci == ri, 1.0, 0.0)[None]
    bmm = lambda a, b_: jnp.einsum("nij,njk->nik", a.astype(bf16), b_.astype(bf16), preferred_element_type=f32)
    bmm_nt = lambda a, b_: jnp.einsum("nik,njk->nij", a.astype(bf16), b_.astype(bf16), preferred_element_type=f32)

    decay = jnp.exp(jnp.where(incl, gcol - grow, NEG))
    eg = jnp.exp(gcol)
    kb = k * bcol
    a_low = jnp.where(strict, bmm_nt(kb, k) * decay, 0.0)
    blk = min(16, c)
    rb, cb_ = ri // blk, ci // blk
    pw = jnp.where((rb == cb_)[None], -a_low, 0.0)
    tmat = eye + pw
    for _ in range(int(math.log2(blk)) - 1):
        pw = bmm(pw, pw)
        tmat = tmat + bmm(tmat, pw)
    while blk < c:
        off = ((ri // (2 * blk)) == (ci // (2 * blk))) & (rb != cb_)
        tmat = tmat - bmm(tmat, bmm(jnp.where(off[None], a_low, 0.0), tmat))
        blk *= 2
        rb, cb_ = ri // blk, ci // blk
    u = bmm(tmat, v * bcol)
    w = bmm(tmat, kb * eg)
    qk = jnp.where(incl, bmm_nt(q, k) * decay, 0.0)
    s = s_ref[...].reshape(nch, DK_B, DV_B)
    v_new = u - bmm(w, s)
    o = bmm(q * eg, s) + bmm(qk, v_new)
    glast = gcol[:, c - 1:c, :]
    k_dec = (k * jnp.exp(glast - gcol)).astype(bf16)
    vb = v_new.astype(bf16)
    upd = jnp.stack([_dot_tn(k_dec[n], vb[n]) for n in range(nch)])
    s_ref[...] = (s * jnp.exp(glast) + upd).reshape(nb, H_B, DK_B, DV_B)
    y = _gated_out(o, z, gn_ref[...][None])
    for b in range(nb):
        for h in range(H_B):
            o_ref[b, :, h * DV_B:(h + 1) * DV_B] = y[b * H_B + h]


def _gdn_prompt(proj, w_conv, gate_params, gdn_norm, c):
    bsz, t, _ = proj.shape
    const = lambda shape: pl.BlockSpec(shape, lambda ti: (0,) * len(shape))
    return pl.pallas_call(
        functools.partial(_gdn_prompt_kernel, nb=bsz, c=c),
        grid=(t // c,),
        in_specs=[pl.BlockSpec((bsz, c, CB_CH), lambda ti: (0, ti, COL_CB * LANE // CB_CH)),
                  pl.BlockSpec((bsz, c, WB_V), lambda ti: (0, ti, COL_ZB * LANE // WB_V)),
                  pl.BlockSpec((bsz, c, LANE), lambda ti: (0, ti, COL_AB)),
                  const((CONV_B, CB_CH)), const((8, LANE)), const((1, DV_B))],
        out_specs=[pl.BlockSpec((bsz, c, WB_V), lambda ti: (0, ti, 0)),
                   pl.BlockSpec((bsz, H_B, DK_B, DV_B), lambda ti: (0, 0, 0, 0))],
        out_shape=[jax.ShapeDtypeStruct((bsz, t, WB_V), f32),
                   jax.ShapeDtypeStruct((bsz, H_B, DK_B, DV_B), f32)],
        scratch_shapes=[pltpu.VMEM((bsz, c + 8, CB_CH), f32), pltpu.VMEM((bsz, c, CB_CH), f32)],
        compiler_params=_cparams(1),
        name="gdn_prompt",
    )(proj, proj, proj, w_conv, gate_params, gdn_norm)


def _gdn_sample_kernel(x_ref, c0_ref, c1_ref, c2_ref, z_ref, ab_ref, s_ref, wconv_ref, gp_ref, gn_ref,
                       o_ref, so_ref, *, bs):
    y = (c0_ref[...] * wconv_ref[0:1, :] + c1_ref[...] * wconv_ref[1:2, :] + c2_ref[...] * wconv_ref[2:3, :]
         + x_ref[...] * wconv_ref[3:4, :])
    act = _silu(y)
    g, beta = _gdn_gates(ab_ref[...], gp_ref)
    eg = jnp.exp(g)
    gn = gn_ref[...]
    rowi = lax.broadcasted_iota(jnp.int32, (bs, DK_B), 0)
    for h in range(H_B):
        q = _l2norm(act[:, h * DK_B:(h + 1) * DK_B]) * (DK_B ** -0.5)
        k = _l2norm(act[:, WB_QK + h * DK_B:WB_QK + (h + 1) * DK_B])
        v = act[:, 2 * WB_QK + h * DV_B:2 * WB_QK + (h + 1) * DV_B]
        egc = eg[:, h:h + 1]
        bc = beta[:, H_B + h:H_B + h + 1]
        qg = q * egc
        ks_rows, qs_rows = [], []
        for b in range(bs):
            lhs = jnp.where(rowi == 0, jnp.broadcast_to(k[b:b + 1], (bs, DK_B)),
                            jnp.where(rowi == 1, jnp.broadcast_to(qg[b:b + 1], (bs, DK_B)), 0.0))
            r = _dot(lhs.astype(bf16), s_ref[b, h].astype(bf16))
            ks_rows.append(r[0:1])
            qs_rows.append(r[1:2])
        ks = jnp.concatenate(ks_rows, axis=0)
        qs = jnp.concatenate(qs_rows, axis=0)
        v_new = bc * v - (bc * egc) * ks
        qk = jnp.sum(q * k, axis=-1, keepdims=True)
        o = qs + qk * v_new
        o_ref[:, h * DV_B:(h + 1) * DV_B] = _gated_out(o, z_ref[:, h * DV_B:(h + 1) * DV_B], gn)
        vb = v_new.astype(bf16)
        for b in range(bs):
            km = jnp.where(rowi == b, k, 0.0).astype(bf16)
            so_ref[b, h] = s_ref[b, h] * egc[b:b + 1, :] + _dot_tn(km, vb)


def _gdn_sample(layer, proj_s, conv_state, ssm_state, w_conv, gate_params, gdn_norm, bs):
    s = proj_s.shape[0]
    const = lambda shape: pl.BlockSpec(shape, lambda i: (0,) * len(shape))
    cst = lambda j: pl.BlockSpec((bs, CB_CH), lambda i: (i, j))
    st = pl.BlockSpec((bs, H_B, DK_B, DV_B), lambda i: (i, 0, 0, 0))
    st_in = pl.BlockSpec((None, bs, H_B, DK_B, DV_B), lambda i: (layer, i, 0, 0, 0))
    return pl.pallas_call(
        functools.partial(_gdn_sample_kernel, bs=bs),
        grid=(s // bs,),
        in_specs=[pl.BlockSpec((bs, CB_CH), lambda i: (i, COL_CB * LANE // CB_CH)),
                  cst(0), cst(1), cst(2),
                  pl.BlockSpec((bs, WB_V), lambda i: (i, COL_ZB * LANE // WB_V)),
                  pl.BlockSpec((bs, LANE), lambda i: (i, COL_AB)),
                  st_in, const((CONV_B, CB_CH)), const((8, LANE)), const((1, DV_B))],
        out_specs=[pl.BlockSpec((bs, WB_V), lambda i: (i, 0)), st],
        out_shape=[jax.ShapeDtypeStruct((s, WB_V), f32),
                   jax.ShapeDtypeStruct(ssm_state.shape[1:], f32)],
        compiler_params=_cparams(1),
        name="gdn_sample",
    )(proj_s, conv_state, conv_state, conv_state, proj_s, proj_s, ssm_state, w_conv, gate_params, gdn_norm)


def _merge_kernel(x_ref, g_ref, oa_ref, ob_ref, oc_ref, wa_ref, wb_ref, wc_ref, wo_ref, y_ref):
    d = D_MODEL
    m = _sigmoid(g_ref[:, 0:d]) * _dot(oa_ref[...].astype(bf16), wa_ref[...])
    m = m + _sigmoid(g_ref[:, d:2 * d]) * _dot(ob_ref[...].astype(bf16), wb_ref[...])
    m = m + _sigmoid(g_ref[:, 2 * d:3 * d]) * _dot(oc_ref[...].astype(bf16), wc_ref[...])
    y_ref[...] = x_ref[...] + _dot(m.astype(bf16), wo_ref[...])


def _merge(x, proj, o_a, o_b, o_c, wa, wb, wc, wo, tm):
    m, d = x.shape
    row = lambda w: pl.BlockSpec((tm, w), lambda i: (i, 0))
    const = lambda a: pl.BlockSpec(a.shape, lambda i: (0, 0))
    return pl.pallas_call(
        _merge_kernel,
        grid=(m // tm,),
        in_specs=[row(d), row(3 * d), row(WA), row(WB_V), row(WC_Q), const(wa), const(wb), const(wc), const(wo)],
        out_specs=row(d),
        out_shape=jax.ShapeDtypeStruct((m, d), f32),
        compiler_params=_cparams(1),
        name="merge_out_proj",
    )(x, proj, o_a, o_b, o_c, wa, wb, wc, wo)


def _ffn_prompt_kernel(x_ref, g_ref, wg_ref, wu_ref, cg_ref, cu_ref, bg_ref, bu_ref, wd_ref,
                       y_ref, sg_ref, su_ref, h_ref, bufg_ref, bufu_ref, carg_ref, caru_ref, *, tm):
    ti = pl.program_id(1)
    j = pl.program_id(2)
    kw = CONV_FFN

    @pl.when(j == 0)
    def _():
        x = x_ref[...]
        ms = jnp.mean(x * x, axis=-1, keepdims=True)
        h_ref[...] = (x * lax.rsqrt(ms + EPS) * g_ref[...]).astype(bf16)
        y_ref[...] = x

    @pl.when(ti == 0)
    def _():
        carg_ref[j] = jnp.zeros(carg_ref.shape[1:], f32)
        caru_ref[j] = jnp.zeros(caru_ref.shape[1:], f32)

    def conv(w_ref, cw_ref, b_ref, buf_ref, car_ref, st_ref):
        buf_ref[0:8, :] = car_ref[j]
        buf_ref[8:8 + tm, :] = _dot(h_ref[...], w_ref[...])
        y = b_ref[...] + buf_ref[pl.ds(8 - (kw - 1), tm), :] * cw_ref[0:1, :]
        for jw in range(1, kw):
            y = y + buf_ref[pl.ds(8 - (kw - 1) + jw, tm), :] * cw_ref[jw:jw + 1, :]
        car_ref[j] = buf_ref[tm:tm + 8, :]
        st_ref[...] = buf_ref[tm + 8 - (kw - 1):tm + 8, :]
        return y

    gate = conv(wg_ref, cg_ref, bg_ref, bufg_ref, carg_ref, sg_ref)
    up = conv(wu_ref, cu_ref, bu_ref, bufu_ref, caru_ref, su_ref)
    act = (_gelu_tanh(gate) * up).astype(bf16)
    y_ref[...] += _dot(act, wd_ref[...])


def _ffn_prompt(x, g, w_up, w_conv, b_conv, w_down, tm, tf):
    bsz, t, d = x.shape
    nj = D_FF // tf
    kw = CONV_FFN
    gcol = lambda shape: pl.BlockSpec(shape, lambda b, ti, j: (0, j))
    ucol = lambda shape: pl.BlockSpec(shape, lambda b, ti, j: (0, nj + j))
    y, sg, su = pl.pallas_call(
        functools.partial(_ffn_prompt_kernel, tm=tm),
        grid=(bsz, t // tm, nj),
        in_specs=[pl.BlockSpec((None, tm, d), lambda b, ti, j: (b, ti, 0)),
                  pl.BlockSpec((1, d), lambda b, ti, j: (0, 0)),
                  gcol((d, tf)), ucol((d, tf)), gcol((kw, tf)), ucol((kw, tf)), gcol((1, tf)), ucol((1, tf)),
                  pl.BlockSpec((tf, d), lambda b, ti, j: (j, 0))],
        out_specs=[pl.BlockSpec((None, tm, d), lambda b, ti, j: (b, ti, 0)),
                   pl.BlockSpec((None, None, kw - 1, tf), lambda b, ti, j: (b, ti, 0, j)),
                   pl.BlockSpec((None, None, kw - 1, tf), lambda b, ti, j: (b, ti, 0, j))],
        out_shape=[jax.ShapeDtypeStruct((bsz, t, d), f32),
                   jax.ShapeDtypeStruct((bsz, t // tm, kw - 1, D_FF), f32),
                   jax.ShapeDtypeStruct((bsz, t // tm, kw - 1, D_FF), f32)],
        scratch_shapes=[pltpu.VMEM((tm, d), bf16),
                        pltpu.VMEM((tm + 8, tf), f32), pltpu.VMEM((tm + 8, tf), f32),
                        pltpu.VMEM((nj, 8, tf), f32), pltpu.VMEM((nj, 8, tf), f32)],
        compiler_params=_cparams(3),
        name="conv_ffn_prompt",
    )(x, g, w_up, w_up, w_conv, w_conv, b_conv, b_conv, w_down)
    return y, jnp.concatenate([sg[:, -1], su[:, -1]], axis=-1)


def _ffn_sample_kernel(x_ref, g_ref, wg_ref, wu_ref, cg_ref, cu_ref, bg_ref, bu_ref, wd_ref,
                       pg0_ref, pg1_ref, pu0_ref, pu1_ref, y_ref, ug_ref, uu_ref, h_ref):
    @pl.when(pl.program_id(0) == 0)
    def _():
        x = x_ref[...]
        ms = jnp.mean(x * x, axis=-1, keepdims=True)
        h_ref[...] = (x * lax.rsqrt(ms + EPS) * g_ref[...]).astype(bf16)
        y_ref[...] = x

    def conv(w_ref, cw_ref, b_ref, p0_ref, p1_ref, u_ref):
        u = _dot(h_ref[...], w_ref[...])
        u_ref[...] = u
        return b_ref[...] + p0_ref[...] * cw_ref[0:1, :] + p1_ref[...] * cw_ref[1:2, :] + u * cw_ref[2:3, :]

    gate = conv(wg_ref, cg_ref, bg_ref, pg0_ref, pg1_ref, ug_ref)
    up = conv(wu_ref, cu_ref, bu_ref, pu0_ref, pu1_ref, uu_ref)
    y_ref[...] += _dot((_gelu_tanh(gate) * up).astype(bf16), wd_ref[...])


def _ffn_sample(x, g, w_up, w_conv, b_conv, w_down, state, tf):
    s, d = x.shape
    nj = D_FF // tf
    kw = CONV_FFN
    assert kw == 3
    col = lambda shape, off: pl.BlockSpec(shape, lambda j: (0, off + j))
    y, ug, uu = pl.pallas_call(
        _ffn_sample_kernel,
        grid=(nj,),
        in_specs=[pl.BlockSpec((s, d), lambda j: (0, 0)), pl.BlockSpec((1, d), lambda j: (0, 0)),
                  col((d, tf), 0), col((d, tf), nj), col((kw, tf), 0), col((kw, tf), nj),
                  col((1, tf), 0), col((1, tf), nj),
                  pl.BlockSpec((tf, d), lambda j: (j, 0)),
                  col((s, tf), 0), col((s, tf), 2 * nj), col((s, tf), nj), col((s, tf), 3 * nj)],
        out_specs=[pl.BlockSpec((s, d), lambda j: (0, 0)), col((s, tf), 0), col((s, tf), 0)],
        out_shape=[jax.ShapeDtypeStruct((s, d), f32),
                   jax.ShapeDtypeStruct((s, D_FF), f32), jax.ShapeDtypeStruct((s, D_FF), f32)],
        scratch_shapes=[pltpu.VMEM((s, d), bf16)],
        compiler_params=_cparams(1),
        name="conv_ffn_sample",
    )(x, g, w_up, w_up, w_conv, w_conv, b_conv, b_conv, w_down, state, state, state, state)
    return y, jnp.concatenate([ug, uu], axis=-1)


def _final_norm_kernel(x_ref, g_ref, y_ref):
    x = x_ref[...]
    ms = jnp.mean(x * x, axis=-1, keepdims=True)
    y_ref[...] = x * lax.rsqrt(ms + EPS) * g_ref[...]


def _final_norm(x, g, tm):
    m, d = x.shape
    return pl.pallas_call(
        _final_norm_kernel,
        grid=(m // tm,),
        in_specs=[pl.BlockSpec((tm, d), lambda i: (i, 0)), pl.BlockSpec((1, d), lambda i: (0, 0))],
        out_specs=pl.BlockSpec((tm, d), lambda i: (i, 0)),
        out_shape=jax.ShapeDtypeStruct((m, d), f32),
        compiler_params=_cparams(1),
        name="final_norm",
    )(x, g)


def _pack_w_in(w_in):
    splits = np.cumsum([WA, WA, WA, CB_CH, WB_V, H_B, H_B, WC_Q, WC_KV, WC_KV])
    qa, ka, va, cb, zb, adec, bgate, qc, kc, vc, gates = jnp.split(w_in, splits.tolist(), axis=-1)
    lead = w_in.shape[:-1]
    qc = qc.reshape(lead + (H_C, HD_C))[..., list(C_HEAD_ORDER), :].reshape(lead + (WC_Q,))
    ab = jnp.concatenate([adec, bgate, jnp.zeros(lead + (LANE - 2 * H_B,), w_in.dtype)], axis=-1)
    pad = jnp.zeros(lead + (LANE,), w_in.dtype)
    packed = jnp.concatenate([gates, qa, ka, va, cb, zb, qc, ab, kc, vc, pad], axis=-1)
    assert packed.shape[-1] == N_COLS
    return packed.astype(bf16)


def _pick(n, pref):
    for t in pref:
        if n % t == 0:
            return t
    return n


def kernel(x_prompt, x_sample, cache_a_k, cache_a_v, cache_c_k, cache_c_v, state_b_conv, state_b_ssm, state_ffn_conv, norm_mix, w_in, w_conv_b, a_log, dt_bias, gdn_norm, sinks, w_br_a, w_br_b, w_br_c, w_out, norm_ffn, w_up, w_conv_ffn, b_conv_ffn, w_down, norm_final):
    depth = w_in.shape[0]
    bp, t, d = x_prompt.shape
    ns = x_sample.shape[0]
    assert x_sample.shape[1] == 1 and d == D_MODEL and t % 256 == 0

    slopes_a, slopes_c = _alibi_slopes()
    order = list(C_HEAD_ORDER)
    slopes_c_col = jnp.asarray(slopes_c[order]).reshape(H_C, 1)

    w_in_p = _pack_w_in(w_in)
    w_br_a16 = w_br_a.astype(bf16)
    w_br_b16 = w_br_b.astype(bf16)
    w_br_c16 = w_br_c.reshape(depth, H_C, HD_C, d)[:, order].reshape(depth, WC_Q, d).astype(bf16)
    w_out16 = w_out.astype(bf16)
    w_up16 = w_up.astype(bf16)
    w_down16 = w_down.astype(bf16)
    gate_params = jnp.zeros((depth, 8, LANE), f32)
    gate_params = gate_params.at[:, 0, :H_B].set(a_log.astype(f32)).at[:, 1, :H_B].set(dt_bias.astype(f32))

    tm_p = _pick(bp * t, (2048, 1024, 512, 256))
    tm_ffn = _pick(t, (1024, 512, 256))
    tn = _pick(N_COLS, (512,))
    tf = 512
    bs = _pick(ns, (8,))
    cache_a_kt = jnp.transpose(cache_a_k, (0, 1, 3, 4, 2))
    cache_a_vt = jnp.transpose(cache_a_v, (0, 1, 3, 4, 2))

    xp = x_prompt.reshape(bp * t, d)
    xs = x_sample.reshape(ns, d)
    outs = {k: [] for k in ("akp", "avp", "ckp", "cvp", "bcp", "bsp", "fcp", "aks", "avs", "cks", "cvs", "bcs",
                            "bss", "fcs")}
    ra = min(WIN_A, t)
    rc = min(WIN_C, t)
    for l in range(depth):
        g_mix = norm_mix[l].reshape(1, d)
        sinks_l = sinks[l].astype(f32)
        proj = _norm_matmul(xp, g_mix, w_in_p[l], tm_p, tn)
        proj3 = proj.reshape(bp, t, N_COLS)
        far_w, far_d = A_PATTERNS[-1]
        if (t // far_d) % LANE == 0:
            far = _band_attn(proj3, jnp.asarray(slopes_a), sinks_l, patterns=A_PATTERNS[-1:], stride=far_d,
                             tq=min(t // far_d, 256), stats_out=True, name="band_attn_a_far")
            o_a = _band_attn(proj3, jnp.asarray(slopes_a), sinks_l, patterns=A_PATTERNS[:-1], tq=256, partial=far,
                             name="band_attn_a_near")
        else:
            o_a = _band_attn(proj3, jnp.asarray(slopes_a), sinks_l, patterns=A_PATTERNS, tq=256, name="band_attn_a")
        o_c = _band_attn(proj3, jnp.asarray(slopes_c), sinks_l, patterns=((WIN_C - 1, 1),), tq=128, mixer_c=True,
                         name="band_attn_c")
        o_b, ssm_p = _gdn_prompt(proj3, w_conv_b[l], gate_params[l], gdn_norm[l].reshape(1, DV_B), c=128)
        xp = _merge(xp, proj, o_a.reshape(bp * t, WA), o_b.reshape(bp * t, WB_V), o_c.reshape(bp * t, WC_Q),
                    w_br_a16[l], w_br_b16[l], w_br_c16[l], w_out16[l], 256)
        y3, fc_p = _ffn_prompt(xp.reshape(bp, t, d), norm_ffn[l].reshape(1, d), w_up16[l], w_conv_ffn[l],
                               b_conv_ffn[l].reshape(1, 2 * D_FF), w_down16[l], tm_ffn, tf)
        xp = y3.reshape(bp * t, d)
        ka = proj3[:, t - ra:, COL_KA * LANE:COL_KA * LANE + WA]
        va = proj3[:, t - ra:, COL_VA * LANE:COL_VA * LANE + WA]
        outs["akp"].append(ka.reshape(bp, ra, H_A, HD_A))
        outs["avp"].append(va.reshape(bp, ra, H_A, HD_A))
        outs["ckp"].append(proj3[:, t - rc:, COL_KC * LANE:(COL_KC + 1) * LANE].reshape(bp, rc, KVH_C, HD_C))
        outs["cvp"].append(proj3[:, t - rc:, COL_VC * LANE:(COL_VC + 1) * LANE].reshape(bp, rc, KVH_C, HD_C))
        outs["bcp"].append(proj3[:, t - (CONV_B - 1):, COL_CB * LANE:COL_CB * LANE + CB_CH])
        outs["bsp"].append(ssm_p)
        outs["fcp"].append(fc_p)
        proj_s = _norm_matmul(xs, g_mix, w_in_p[l], ns, tn)
        heads_a = lambda col: proj_s[:, col * LANE:col * LANE + WA].reshape(ns, H_A, HD_A)
        ka_s, va_s = heads_a(COL_KA), heads_a(COL_VA)
        lanes = lambda a: jnp.broadcast_to(a[..., None], a.shape + (LANE,))
        o_a = _attn_a_sample(l, lanes(heads_a(COL_QA)), lanes(ka_s), lanes(va_s), cache_a_kt, cache_a_vt,
                             slopes_a)[..., 0].reshape(ns, WA)
        pc = cache_c_k.shape[2]
        o_c = _attn_c_sample(proj_s, cache_c_k[l].reshape(ns, pc, WC_KV), cache_c_v[l].reshape(ns, pc, WC_KV),
                             slopes_c_col, sinks_l[jnp.asarray(order)].reshape(H_C, 1), bs)
        conv_st = state_b_conv[l]
        o_b, ssm_s = _gdn_sample(l, proj_s, conv_st.reshape(ns, (CONV_B - 1) * CB_CH), state_b_ssm, w_conv_b[l],
                                 gate_params[l], gdn_norm[l].reshape(1, DV_B), bs)
        xs = _merge(xs, proj_s, o_a, o_b, o_c, w_br_a16[l], w_br_b16[l], w_br_c16[l], w_out16[l], ns)
        ffn_st = state_ffn_conv[l]
        xs, u_s = _ffn_sample(xs, norm_ffn[l].reshape(1, d), w_up16[l], w_conv_ffn[l],
                              b_conv_ffn[l].reshape(1, 2 * D_FF), w_down16[l],
                              ffn_st.reshape(ns, (CONV_FFN - 1) * 2 * D_FF), tf)
        outs["aks"].append(ka_s[:, None])
        outs["avs"].append(va_s[:, None])
        outs["cks"].append(proj_s[:, COL_KC * LANE:(COL_KC + 1) * LANE].reshape(ns, 1, KVH_C, HD_C))
        outs["cvs"].append(proj_s[:, COL_VC * LANE:(COL_VC + 1) * LANE].reshape(ns, 1, KVH_C, HD_C))
        xb_new = proj_s[:, COL_CB * LANE:COL_CB * LANE + CB_CH]
        outs["bcs"].append(jnp.concatenate([conv_st[:, 1:], xb_new[:, None, :]], axis=1))
        outs["bss"].append(ssm_s)
        outs["fcs"].append(jnp.concatenate([ffn_st[:, 1:], u_s[:, None, :]], axis=1))

    g_fin = norm_final.reshape(1, d)
    y_prompt = _final_norm(xp, g_fin, tm_p).reshape(bp, t, d)
    y_sample = _final_norm(xs, g_fin, ns).reshape(ns, 1, d)
    st = {k: jnp.stack(v) for k, v in outs.items()}
    return (y_prompt, y_sample, st["akp"], st["avp"], st["ckp"], st["cvp"], st["bcp"], st["bsp"], st["fcp"],
            st["aks"], st["avs"], st["cks"], st["cvs"], st["bcs"], st["bss"], st["fcs"])
```

```python
import functools
import math

import numpy as np
import jax
import jax.numpy as jnp
from jax import lax
from jax.experimental import pallas as pl
from jax.experimental.pallas import tpu as pltpu

D_MODEL = 1024
H_A, HD_A = 8, 64
A_PATTERNS = ((128, 1), (512, 4), (2048, 16))
WIN_A = 2048
H_B, DK_B, DV_B = 4, 128, 128
CONV_B = 4
CHUNK_B = 64
H_C, KVH_C, HD_C = 8, 2, 64
WIN_C = 128
D_FF = 3 * D_MODEL
CONV_FFN = 3
EPS = 1e-6
NEG = -1e30
LOG2E = math.log2(math.e)

WA = H_A * HD_A
WB_QK = H_B * DK_B
WB_V = H_B * DV_B
WC_Q = H_C * HD_C
WC_KV = KVH_C * HD_C
CB_CH = 2 * WB_QK + WB_V

LANE = 128
C_HEAD_ORDER = (0, 4, 1, 5, 2, 6, 3, 7)

COL_GATES = 0
COL_QA, COL_KA, COL_VA = 24, 28, 32
COL_CB = 36
COL_ZB = 48
COL_QC = 52
COL_AB = 56
COL_KC, COL_VC = 57, 58
N_COLS = 60 * LANE

VMEM_LIMIT = 48 * 1024 * 1024

f32 = jnp.float32
bf16 = jnp.bfloat16


def _cparams(n_axes):
    return pltpu.CompilerParams(dimension_semantics=("arbitrary",) * n_axes, vmem_limit_bytes=VMEM_LIMIT)


def _alibi_slopes():
    n = H_A + H_C
    s = (2.0 ** (-8.0 * (np.arange(n) + 1) / n)).astype(np.float32)
    return s[H_C:], s[:H_C]


def _sigmoid(x):
    return 1.0 / (1.0 + jnp.exp(-x))


def _silu(x):
    return x * _sigmoid(x)


def _softplus(x):
    return jnp.maximum(x, 0.0) + jnp.log1p(jnp.exp(-jnp.abs(x)))


def _gelu_tanh(x):
    c = math.sqrt(2.0 / math.pi)
    return 0.5 * x * (1.0 + jnp.tanh(c * (x + 0.044715 * (x * x * x))))


def _dot(a, b):
    return jnp.dot(a, b, preferred_element_type=f32)


def _dot_nt(a, b):
    return lax.dot_general(a, b, (((1,), (1,)), ((), ())), preferred_element_type=f32)


def _dot_tn(a, b):
    return lax.dot_general(a, b, (((0,), (0,)), ((), ())), preferred_element_type=f32)


def _dot_f32(a, b):
    return jnp.dot(a, b, preferred_element_type=f32, precision=lax.Precision.HIGHEST)


def _norm_matmul_kernel(x_ref, g_ref, w_ref, o_ref, h_ref):
    @pl.when(pl.program_id(1) == 0)
    def _():
        x = x_ref[...]
        ms = jnp.mean(x * x, axis=-1, keepdims=True)
        h_ref[...] = (x * lax.rsqrt(ms + EPS) * g_ref[...]).astype(bf16)

    o_ref[...] = _dot(h_ref[...], w_ref[...])


def _norm_matmul(x, g, w, tm, tn):
    m, d = x.shape
    n = w.shape[1]
    return pl.pallas_call(
        _norm_matmul_kernel,
        grid=(m // tm, n // tn),
        in_specs=[pl.BlockSpec((tm, d), lambda i, j: (i, 0)),
                  pl.BlockSpec((1, d), lambda i, j: (0, 0)),
                  pl.BlockSpec((d, tn), lambda i, j: (0, j))],
        out_specs=pl.BlockSpec((tm, tn), lambda i, j: (i, j)),
        out_shape=jax.ShapeDtypeStruct((m, n), f32),
        scratch_shapes=[pltpu.VMEM((tm, d), bf16)],
        compiler_params=_cparams(2),
        name="norm_in_proj",
    )(x, g, w)


def _band_bias(patterns, stride, dist, slope):
    tok = dist * stride
    mult = jnp.zeros(dist.shape, f32)
    for w, d in patterns:
        hit = (tok <= w) & ((tok & (d - 1)) == 0)
        mult = mult + jnp.where(hit, 1.0, 0.0)
    logm = jnp.where(mult > 2.5, math.log(3.0), jnp.where(mult > 1.5, math.log(2.0), 0.0))
    return jnp.where((dist >= 0) & (mult > 0.5), logm - slope * tok.astype(f32), NEG)


def _band_attn_kernel(slope_ref, sink_ref, q_ref, k_ref, v_ref, *rest,
                      tq, nd, nt, head_a, head_t, head_b, patterns, stride, use_sink, scale, variants,
                      merge_in, stats_out):
    if merge_in:
        acc_in_ref, m_in_ref, l_in_ref, *rest = rest
    if stats_out:
        o_ref, m_out_ref, l_out_ref, *rest = rest
    else:
        o_ref, *rest = rest
    if stride == 1:
        bias_ref, s_ref, p_ref, k16_ref, v16_ref = rest
    else:
        bias_ref, s_ref, p_ref = rest
    hp = pl.program_id(0)
    b = pl.program_id(1)
    qi = pl.program_id(2)
    ng = 2 * nt
    strip = 64
    half = ng * tq // 2
    head = lambda tt, hh: hp * head_a + tt * head_t + hh * head_b

    @pl.when((b == 0) & (qi == 0))
    def _():
        row = lax.broadcasted_iota(jnp.int32, (tq, tq), 0)
        col = lax.broadcasted_iota(jnp.int32, (tq, tq), 1)
        for tt in range(nt):
            for hh in range(2):
                rows = slice((2 * tt + hh) * tq, (2 * tt + hh + 1) * tq)
                slope = slope_ref[head(tt, hh)]
                for d in range(nd):
                    bias = _band_bias(patterns, stride, d * tq + row - col, slope)
                    bias_ref[d, rows, :] = jnp.where(bias > 0.5 * NEG, bias * LOG2E, NEG)
        bias_ref[nd] = jnp.full((ng * tq, tq), NEG, f32)

    if stride == 1:
        @pl.when(qi == 0)
        def _():
            k16_ref[...] = k_ref[...].astype(bf16)
            v16_ref[...] = v_ref[...].astype(bf16)

    lane = lax.broadcasted_iota(jnp.int32, (tq, LANE), 1)
    lo = lane < LANE // 2
    rows_of = lambda cls: slice(None) if stride == 1 else pl.ds(cls, tq, stride=stride)

    def attend(n, cls=0):
        blocks = []
        for tt in range(nt):
            q = q_ref[rows_of(cls), tt * LANE:(tt + 1) * LANE] * (scale * LOG2E)
            blocks += [jnp.where(lo, q, 0.0), jnp.where(lo, 0.0, q)]
        qs = jnp.concatenate(blocks, axis=0).astype(bf16)
        w0 = jnp.maximum(qi - (n - 1), 0)
        for c in range(n):
            delta = qi - (w0 + c)
            bidx = jnp.where((delta >= 0) & (delta < nd), delta, nd)
            if stride == 1:
                kt = k16_ref[pl.ds(pl.multiple_of((w0 + c) * tq, tq), tq), :]
            else:
                kt = k_ref[rows_of(cls), :].astype(bf16)
            for hrows in (slice(0, half), slice(half, 2 * half)):
                s_ref[hrows, c * tq:(c + 1) * tq] = _dot_nt(qs[hrows], kt) + bias_ref[bidx, hrows, :]
        nblk = n * tq // LANE
        ls, ms, sinks = [], [], []
        for r in range(ng * tq // strip):
            rows = slice(r * strip, (r + 1) * strip)
            mx = s_ref[rows, 0:LANE]
            for j in range(1, nblk):
                mx = jnp.maximum(mx, s_ref[rows, j * LANE:(j + 1) * LANE])
            m = jnp.broadcast_to(jnp.max(mx, axis=-1, keepdims=True), (strip, LANE))
            if use_sink:
                g = r * strip // tq
                sink = jnp.full((strip, LANE), sink_ref[head(g // 2, g % 2)], f32) * LOG2E
                m = jnp.maximum(m, sink)
                sinks.append(sink)
            ms.append(m)
        for r in range(ng * tq // strip):
            rows = slice(r * strip, (r + 1) * strip)
            m = ms[r]
            lsum = jnp.zeros((strip, LANE), f32)
            for j in range(nblk):
                p = jnp.exp2(s_ref[rows, j * LANE:(j + 1) * LANE] - m)
                lsum = lsum + p
                p_ref[rows, j * LANE:(j + 1) * LANE] = p.astype(bf16)
            l = jnp.broadcast_to(jnp.sum(lsum, axis=-1, keepdims=True), (strip, LANE))
            if use_sink:
                l = l + jnp.exp2(sinks[r] - m)
            ls.append(l)
        if stride == 1:
            vt = v16_ref[pl.ds(pl.multiple_of(w0 * tq, tq), n * tq), :]
        else:
            vt = v_ref[rows_of(cls), :].astype(bf16)
        acc = jnp.concatenate([_dot(p_ref[0:half, 0:n * tq], vt), _dot(p_ref[half:2 * half, 0:n * tq], vt)],
                              axis=0)
        l = jnp.concatenate(ls, axis=0)
        m = jnp.concatenate(ms, axis=0)
        halves = lambda a, tt: jnp.where(lo, a[2 * tt * tq:(2 * tt + 1) * tq], a[(2 * tt + 1) * tq:(2 * tt + 2) * tq])
        for tt in range(nt):
            cols = slice(tt * LANE, (tt + 1) * LANE)
            acc_t = halves(acc, tt)
            m_t = halves(m, tt)
            l_t = halves(l, tt)
            if merge_in:
                m_in = m_in_ref[:, cols]
                m_tot = jnp.maximum(m_t, m_in)
                a, a_in = jnp.exp2(m_t - m_tot), jnp.exp2(m_in - m_tot)
                acc_t = acc_t * a + acc_in_ref[:, cols] * a_in
                l_t = l_t * a + l_in_ref[:, cols] * a_in
                m_t = m_tot
            if stats_out:
                o_ref[rows_of(cls), cols] = acc_t
                m_out_ref[rows_of(cls), cols] = m_t
                l_out_ref[rows_of(cls), cols] = l_t
            else:
                o_ref[rows_of(cls), cols] = acc_t / l_t

    if stride > 1:
        for cls in range(stride):
            attend(1, cls)
        return
    prev = 0
    for n in variants:
        cond = (qi >= prev) if n == variants[-1] else ((qi >= prev) & (qi < n))
        pl.when(cond)(functools.partial(attend, n))
        prev = n


def _band_attn(proj, slopes, sinks, *, patterns, tq, stride=1, partial=None, stats_out=False, mixer_c=False, name):
    bsz, t, ncols = proj.shape
    length = t // stride
    assert stride == 1 or (length == tq and partial is None)
    if mixer_c:
        nhp, nt = 1, 4
        head_a, head_t, head_b = 0, 1, 4
        qcol, kcol, vcol = COL_QC // 4, COL_KC, COL_VC
        hp_k, hd = 0, HD_C
    else:
        nhp, nt = 4, 1
        head_a, head_t, head_b = 2, 0, 1
        qcol, kcol, vcol = COL_QA, COL_KA, COL_VA
        hp_k, hd = 1, HD_A
    span = max(w for w, _ in patterns) // stride
    nd = min(-(-span // tq) + 1, length // tq)
    variants = tuple(sorted({-(-nd // 3), -(-2 * nd // 3), nd}))
    kern = functools.partial(_band_attn_kernel, tq=tq, nd=nd, nt=nt, head_a=head_a, head_t=head_t, head_b=head_b,
                             patterns=patterns, stride=stride, use_sink=mixer_c, scale=hd ** -0.5,
                             variants=variants, merge_in=partial is not None, stats_out=stats_out)
    rows = tq * stride
    smem = pl.BlockSpec(memory_space=pltpu.SMEM)
    in_specs = [smem, smem,
                pl.BlockSpec((None, rows, nt * LANE), lambda hp, b, qi: (b, qi, qcol + hp)),
                pl.BlockSpec((None, t, LANE), lambda hp, b, qi: (b, 0, kcol + hp * hp_k)),
                pl.BlockSpec((None, t, LANE), lambda hp, b, qi: (b, 0, vcol + hp * hp_k))]
    operands = [slopes, sinks, proj, proj, proj]
    tile = pl.BlockSpec((None, rows, nt * LANE), lambda hp, b, qi: (b, qi, hp))
    if partial is not None:
        in_specs += [tile] * 3
        operands += list(partial)
    scratch = [pltpu.VMEM((nd + 1, 2 * nt * tq, tq), f32), pltpu.VMEM((2 * nt * tq, nd * tq), f32),
               pltpu.VMEM((2 * nt * tq, nd * tq), bf16)]
    if stride == 1:
        scratch += [pltpu.VMEM((t, LANE), bf16), pltpu.VMEM((t, LANE), bf16)]
    n_out = 3 if stats_out else 1
    outs = pl.pallas_call(
        kern,
        grid=(nhp, bsz, t // rows),
        in_specs=in_specs,
        out_specs=[tile] * n_out,
        out_shape=[jax.ShapeDtypeStruct((bsz, t, 4 * LANE), f32)] * n_out,
        scratch_shapes=scratch,
        compiler_params=_cparams(3),
        name=name,
    )(*operands)
    return outs if stats_out else outs[0]


def _attn_a_sample_kernel(bias_ref, q_ref, kn_ref, vn_ref, kt_ref, vt_ref, o_ref, *, scale):
    nh, hd, p = kt_ref.shape
    for h in range(nh):
        qc = q_ref[h] * scale
        s_new = jnp.sum(qc * kn_ref[h], axis=0, keepdims=True)[:, 0:1]
        s = jnp.concatenate(
            [jnp.sum(kt_ref[h, :, j * LANE:(j + 1) * LANE] * qc, axis=0, keepdims=True) for j in range(p // LANE)],
            axis=1) + bias_ref[h:h + 1, :]
        m = jnp.maximum(jnp.max(s, axis=-1, keepdims=True), s_new)
        pr = jnp.exp(s - m)
        p_new = float(len(A_PATTERNS)) * jnp.exp(s_new - m)
        l = jnp.sum(pr, axis=-1, keepdims=True) + p_new
        acc = vt_ref[h, :, 0:LANE] * pr[:, 0:LANE]
        for j in range(1, p // LANE):
            acc = acc + vt_ref[h, :, j * LANE:(j + 1) * LANE] * pr[:, j * LANE:(j + 1) * LANE]
        o = (jnp.sum(acc, axis=-1, keepdims=True) + p_new * vn_ref[h]) / l
        o_ref[h] = o


def _attn_a_sample(layer, q, kn, vn, cache_kt, cache_vt, slopes):
    _, s, h, hd, p = cache_kt.shape
    dist = (p - np.arange(p)).astype(np.int64)
    mult = sum(((dist <= wd) & (dist % dl == 0)).astype(np.float64) for wd, dl in A_PATTERNS)
    slopes = np.asarray(slopes, np.float32)
    bias = np.where(mult[None] > 0, np.log(np.maximum(mult, 1.0))[None].astype(np.float32)
                    - slopes[:, None] * dist[None].astype(np.float32), np.float32(NEG)).astype(np.float32)
    tok = pl.BlockSpec((None, h, hd, LANE), lambda i: (i, 0, 0, 0))
    cache = pl.BlockSpec((None, None, h, hd, p), lambda i: (layer, i, 0, 0, 0))
    return pl.pallas_call(
        functools.partial(_attn_a_sample_kernel, scale=hd ** -0.5),
        grid=(s,),
        in_specs=[pl.BlockSpec(bias.shape, lambda i: (0, 0)), tok, tok, tok, cache, cache],
        out_specs=tok,
        out_shape=jax.ShapeDtypeStruct((s, h, hd, LANE), f32),
        compiler_params=_cparams(1),
        name="attn_a_sample",
    )(jnp.asarray(bias), q, kn, vn, cache_kt, cache_vt)


def _attn_c_sample_kernel(slope_ref, sink_ref, q_ref, kn_ref, vn_ref, k_ref, v_ref, o_ref, *, bs, scale):
    rows = k_ref.shape[1]
    prow = lax.broadcasted_iota(jnp.int32, (H_C, LANE), 0)
    plane = lax.broadcasted_iota(jnp.int32, (H_C, LANE), 1)
    halfmask = (plane // HD_C) == (prow % 2)
    j = lax.broadcasted_iota(jnp.int32, (H_C, rows), 1)
    dist = rows - j
    slope = slope_ref[...]
    sink = sink_ref[...]
    bias = jnp.where(dist < WIN_C, -slope * dist.astype(f32), NEG)
    for b in range(bs):
        tiles = []
        for t in range(WC_Q // LANE):
            qt = q_ref[b:b + 1, t * LANE:(t + 1) * LANE] * scale
            tiles += [qt, qt]
        qblk = jnp.where(halfmask, jnp.concatenate(tiles, axis=0), 0.0)
        kn = kn_ref[b:b + 1, :]
        vn = vn_ref[b:b + 1, :]
        s_new = jnp.sum(qblk * kn, axis=-1, keepdims=True)
        s = _dot_nt(qblk.astype(bf16), k_ref[b].astype(bf16)) + bias
        m = jnp.maximum(jnp.maximum(s_new, jnp.max(s, axis=-1, keepdims=True)), sink)
        p = jnp.exp(s - m)
        p_new = jnp.exp(s_new - m)
        l = jnp.sum(p, axis=-1, keepdims=True) + p_new + jnp.exp(sink - m)
        acc = _dot(p.astype(bf16), v_ref[b].astype(bf16)) + p_new * vn
        o = jnp.where(halfmask, acc / l, 0.0)
        for t in range(WC_Q // LANE):
            o_ref[b:b + 1, t * LANE:(t + 1) * LANE] = o[2 * t:2 * t + 1, :] + o[2 * t + 1:2 * t + 2, :]


def _attn_c_sample(proj_s, cache_k, cache_v, slopes, sinks, bs):
    s, p, w = cache_k.shape
    assert p == WIN_C and w == LANE
    col = pl.BlockSpec((H_C, 1), lambda i: (0, 0))
    cache = pl.BlockSpec((bs, p, w), lambda i: (i, 0, 0))
    return pl.pallas_call(
        functools.partial(_attn_c_sample_kernel, bs=bs, scale=HD_C ** -0.5),
        grid=(s // bs,),
        in_specs=[col, col,
                  pl.BlockSpec((bs, WC_Q), lambda i: (i, COL_QC * LANE // WC_Q)),
                  pl.BlockSpec((bs, LANE), lambda i: (i, COL_KC)),
                  pl.BlockSpec((bs, LANE), lambda i: (i, COL_VC)),
                  cache, cache],
        out_specs=pl.BlockSpec((bs, WC_Q), lambda i: (i, 0)),
        out_shape=jax.ShapeDtypeStruct((s, WC_Q), f32),
        compiler_params=_cparams(1),
        name="attn_c_sample",
    )(slopes, sinks, proj_s, proj_s, proj_s, cache_k, cache_v)


def _gdn_gates(ab, gp_ref):
    g = -jnp.exp(gp_ref[0:1, :]) * _softplus(ab + gp_ref[1:2, :])
    return g, _sigmoid(ab)


def _l2norm(x):
    return x * lax.rsqrt(jnp.sum(x * x, axis=-1, keepdims=True) + EPS)


def _gated_out(o, z, gn):
    y = o * lax.rsqrt(jnp.mean(o * o, axis=-1, keepdims=True) + EPS) * gn
    return y * _silu(z)


def _gdn_prompt_kernel(x_ref, z_ref, ab_ref, wconv_ref, gp_ref, gn_ref, o_ref, s_ref, buf_ref, act_ref, *, nb, c):
    ti = pl.program_id(0)
    nch = nb * H_B

    @pl.when(ti == 0)
    def _():
        buf_ref[:, 0:8, :] = jnp.zeros((nb, 8, CB_CH), f32)
        s_ref[...] = jnp.zeros(s_ref.shape, f32)

    for b in range(nb):
        buf_ref[b, 8:8 + c, :] = x_ref[b]
        for cb in range(CB_CH // LANE):
            sl = slice(cb * LANE, (cb + 1) * LANE)
            y = buf_ref[b, pl.ds(8 - (CONV_B - 1), c), sl] * wconv_ref[0:1, sl]
            for jw in range(1, CONV_B):
                y = y + buf_ref[b, pl.ds(8 - (CONV_B - 1) + jw, c), sl] * wconv_ref[jw:jw + 1, sl]
            act_ref[b, :, sl] = _silu(y)
        buf_ref[b, 0:8, :] = buf_ref[b, c:c + 8, :]

    g, beta = _gdn_gates(ab_ref[...].reshape(nb * c, LANE), gp_ref)
    rowc = lax.broadcasted_iota(jnp.int32, (nb * c, LANE), 0) & (c - 1)
    sh = 1
    while sh < c:
        g = g + jnp.where(rowc >= sh, pltpu.roll(g, sh, axis=0), 0.0)
        sh *= 2

    qs, ks, vs, zs, gcols, grows, bcols = [], [], [], [], [], [], []
    for b in range(nb):
        gb = g[b * c:(b + 1) * c]
        gb_t = jnp.transpose(gb)
        for h in range(H_B):
            qs.append(_l2norm(act_ref[b, :, h * DK_B:(h + 1) * DK_B]) * (DK_B ** -0.5))
            ks.append(_l2norm(act_ref[b, :, WB_QK + h * DK_B:WB_QK + (h + 1) * DK_B]))
            vs.append(act_ref[b, :, 2 * WB_QK + h * DV_B:2 * WB_QK + (h + 1) * DV_B])
            zs.append(z_ref[b, :, h * DV_B:(h + 1) * DV_B])
            gcols.append(gb[:, h:h + 1])
            grows.append(gb_t[h:h + 1, :])
            bcols.append(beta[b * c:(b + 1) * c, H_B + h:H_B + h + 1])
    q, k, v, z = jnp.stack(qs), jnp.stack(ks), jnp.stack(vs), jnp.stack(zs)
    gcol, grow, bcol = jnp.stack(gcols), jnp.stack(grows), jnp.stack(bcols)

    ri = lax.broadcasted_iota(jnp.int32, (c, c), 0)
    ci = lax.broadcasted_iota(jnp.int32, (c, c), 1)
    incl = (ci <= ri)[None]
    strict = (ci < ri)[None]
    eye = jnp.where(ci == ri, 1.0, 0.0)[None]
    bmm = lambda a, b_: jnp.einsum("nij,njk->nik", a.astype(bf16), b_.astype(bf16), preferred_element_type=f32)
    bmm_nt = lambda a, b_: jnp.einsum("nik,njk->nij", a.astype(bf16), b_.astype(bf16), preferred_element_type=f32)

    decay = jnp.exp(jnp.where(incl, gcol - grow, NEG))
    eg = jnp.exp(gcol)
    kb = k * bcol
    a_low = jnp.where(strict, bmm_nt(kb, k) * decay, 0.0)
    blk = min(16, c)
    rb, cb_ = ri // blk, ci // blk
    pw = jnp.where((rb == cb_)[None], -a_low, 0.0)
    tmat = eye + pw
    for _ in range(int(math.log2(blk)) - 1):
        pw = bmm(pw, pw)
        tmat = tmat + bmm(tmat, pw)
    while blk < c:
        off = ((ri // (2 * blk)) == (ci // (2 * blk))) & (rb != cb_)
        tmat = tmat - bmm(tmat, bmm(jnp.where(off[None], a_low, 0.0), tmat))
        blk *= 2
        rb, cb_ = ri // blk, ci // blk
    u = bmm(tmat, v * bcol)
    w = bmm(tmat, kb * eg)
    qk = jnp.where(incl, bmm_nt(q, k) * decay, 0.0)
    s = s_ref[...].reshape(nch, DK_B, DV_B)
    v_new = u - bmm(w, s)
    o = bmm(q * eg, s) + bmm(qk, v_new)
    glast = gcol[:, c - 1:c, :]
    k_dec = (k * jnp.exp(glast - gcol)).astype(bf16)
    vb = v_new.astype(bf16)
    upd = jnp.stack([_dot_tn(k_dec[n], vb[n]) for n in range(nch)])
    s_ref[...] = (s * jnp.exp(glast) + upd).reshape(nb, H_B, DK_B, DV_B)
    y = _gated_out(o, z, gn_ref[...][None])
    for b in range(nb):
        for h in range(H_B):
            o_ref[b, :, h * DV_B:(h + 1) * DV_B] = y[b * H_B + h]


def _gdn_prompt(proj, w_conv, gate_params, gdn_norm, c):
    bsz, t, _ = proj.shape
    const = lambda shape: pl.BlockSpec(shape, lambda ti: (0,) * len(shape))
    return pl.pallas_call(
        functools.partial(_gdn_prompt_kernel, nb=bsz, c=c),
        grid=(t // c,),
        in_specs=[pl.BlockSpec((bsz, c, CB_CH), lambda ti: (0, ti, COL_CB * LANE // CB_CH)),
                  pl.BlockSpec((bsz, c, WB_V), lambda ti: (0, ti, COL_ZB * LANE // WB_V)),
                  pl.BlockSpec((bsz, c, LANE), lambda ti: (0, ti, COL_AB)),
                  const((CONV_B, CB_CH)), const((8, LANE)), const((1, DV_B))],
        out_specs=[pl.BlockSpec((bsz, c, WB_V), lambda ti: (0, ti, 0)),
                   pl.BlockSpec((bsz, H_B, DK_B, DV_B), lambda ti: (0, 0, 0, 0))],
        out_shape=[jax.ShapeDtypeStruct((bsz, t, WB_V), f32),
                   jax.ShapeDtypeStruct((bsz, H_B, DK_B, DV_B), f32)],
        scratch_shapes=[pltpu.VMEM((bsz, c + 8, CB_CH), f32), pltpu.VMEM((bsz, c, CB_CH), f32)],
        compiler_params=_cparams(1),
        name="gdn_prompt",
    )(proj, proj, proj, w_conv, gate_params, gdn_norm)


def _gdn_sample_kernel(x_ref, c0_ref, c1_ref, c2_ref, z_ref, ab_ref, s_ref, wconv_ref, gp_ref, gn_ref,
                       o_ref, so_ref, *, bs):
    y = (c0_ref[...] * wconv_ref[0:1, :] + c1_ref[...] * wconv_ref[1:2, :] + c2_ref[...] * wconv_ref[2:3, :]
         + x_ref[...] * wconv_ref[3:4, :])
    act = _silu(y)
    g, beta = _gdn_gates(ab_ref[...], gp_ref)
    eg = jnp.exp(g)
    gn = gn_ref[...]
    rowi = lax.broadcasted_iota(jnp.int32, (bs, DK_B), 0)
    for h in range(H_B):
        q = _l2norm(act[:, h * DK_B:(h + 1) * DK_B]) * (DK_B ** -0.5)
        k = _l2norm(act[:, WB_QK + h * DK_B:WB_QK + (h + 1) * DK_B])
        v = act[:, 2 * WB_QK + h * DV_B:2 * WB_QK + (h + 1) * DV_B]
        egc = eg[:, h:h + 1]
        bc = beta[:, H_B + h:H_B + h + 1]
        qg = q * egc
        ks_rows, qs_rows = [], []
        for b in range(bs):
            lhs = jnp.where(rowi == 0, jnp.broadcast_to(k[b:b + 1], (bs, DK_B)),
                            jnp.where(rowi == 1, jnp.broadcast_to(qg[b:b + 1], (bs, DK_B)), 0.0))
            r = _dot(lhs.astype(bf16), s_ref[b, h].astype(bf16))
            ks_rows.append(r[0:1])
            qs_rows.append(r[1:2])
        ks = jnp.concatenate(ks_rows, axis=0)
        qs = jnp.concatenate(qs_rows, axis=0)
        v_new = bc * v - (bc * egc) * ks
        qk = jnp.sum(q * k, axis=-1, keepdims=True)
        o = qs + qk * v_new
        o_ref[:, h * DV_B:(h + 1) * DV_B] = _gated_out(o, z_ref[:, h * DV_B:(h + 1) * DV_B], gn)
        vb = v_new.astype(bf16)
        for b in range(bs):
            km = jnp.where(rowi == b, k, 0.0).astype(bf16)
            so_ref[b, h] = s_ref[b, h] * egc[b:b + 1, :] + _dot_tn(km, vb)


def _gdn_sample(layer, proj_s, conv_state, ssm_state, w_conv, gate_params, gdn_norm, bs):
    s = proj_s.shape[0]
    const = lambda shape: pl.BlockSpec(shape, lambda i: (0,) * len(shape))
    cst = lambda j: pl.BlockSpec((bs, CB_CH), lambda i: (i, j))
    st = pl.BlockSpec((bs, H_B, DK_B, DV_B), lambda i: (i, 0, 0, 0))
    st_in = pl.BlockSpec((None, bs, H_B, DK_B, DV_B), lambda i: (layer, i, 0, 0, 0))
    return pl.pallas_call(
        functools.partial(_gdn_sample_kernel, bs=bs),
        grid=(s // bs,),
        in_specs=[pl.BlockSpec((bs, CB_CH), lambda i: (i, COL_CB * LANE // CB_CH)),
                  cst(0), cst(1), cst(2),
                  pl.BlockSpec((bs, WB_V), lambda i: (i, COL_ZB * LANE // WB_V)),
                  pl.BlockSpec((bs, LANE), lambda i: (i, COL_AB)),
                  st_in, const((CONV_B, CB_CH)), const((8, LANE)), const((1, DV_B))],
        out_specs=[pl.BlockSpec((bs, WB_V), lambda i: (i, 0)), st],
        out_shape=[jax.ShapeDtypeStruct((s, WB_V), f32),
                   jax.ShapeDtypeStruct(ssm_state.shape[1:], f32)],
        compiler_params=_cparams(1),
        name="gdn_sample",
    )(proj_s, conv_state, conv_state, conv_state, proj_s, proj_s, ssm_state, w_conv, gate_params, gdn_norm)


def _merge_kernel(x_ref, g_ref, oa_ref, ob_ref, oc_ref, wa_ref, wb_ref, wc_ref, wo_ref, y_ref):
    d = D_MODEL
    m = _sigmoid(g_ref[:, 0:d]) * _dot(oa_ref[...].astype(bf16), wa_ref[...])
    m = m + _sigmoid(g_ref[:, d:2 * d]) * _dot(ob_ref[...].astype(bf16), wb_ref[...])
    m = m + _sigmoid(g_ref[:, 2 * d:3 * d]) * _dot(oc_ref[...].astype(bf16), wc_ref[...])
    y_ref[...] = x_ref[...] + _dot(m.astype(bf16), wo_ref[...])


def _merge(x, proj, o_a, o_b, o_c, wa, wb, wc, wo, tm):
    m, d = x.shape
    row = lambda w: pl.BlockSpec((tm, w), lambda i: (i, 0))
    const = lambda a: pl.BlockSpec(a.shape, lambda i: (0, 0))
    return pl.pallas_call(
        _merge_kernel,
        grid=(m // tm,),
        in_specs=[row(d), row(3 * d), row(WA), row(WB_V), row(WC_Q), const(wa), const(wb), const(wc), const(wo)],
        out_specs=row(d),
        out_shape=jax.ShapeDtypeStruct((m, d), f32),
        compiler_params=_cparams(1),
        name="merge_out_proj",
    )(x, proj, o_a, o_b, o_c, wa, wb, wc, wo)


def _ffn_prompt_kernel(x_ref, g_ref, wg_ref, wu_ref, cg_ref, cu_ref, bg_ref, bu_ref, wd_ref,
                       y_ref, sg_ref, su_ref, h_ref, bufg_ref, bufu_ref, carg_ref, caru_ref, *, tm):
    ti = pl.program_id(1)
    j = pl.program_id(2)
    kw = CONV_FFN

    @pl.when(j == 0)
    def _():
        x = x_ref[...]
        ms = jnp.mean(x * x, axis=-1, keepdims=True)
        h_ref[...] = (x * lax.rsqrt(ms + EPS) * g_ref[...]).astype(bf16)
        y_ref[...] = x

    @pl.when(ti == 0)
    def _():
        carg_ref[j] = jnp.zeros(carg_ref.shape[1:], f32)
        caru_ref[j] = jnp.zeros(caru_ref.shape[1:], f32)

    def conv(w_ref, cw_ref, b_ref, buf_ref, car_ref, st_ref):
        buf_ref[0:8, :] = car_ref[j]
        buf_ref[8:8 + tm, :] = _dot(h_ref[...], w_ref[...])
        y = b_ref[...] + buf_ref[pl.ds(8 - (kw - 1), tm), :] * cw_ref[0:1, :]
        for jw in range(1, kw):
            y = y + buf_ref[pl.ds(8 - (kw - 1) + jw, tm), :] * cw_ref[jw:jw + 1, :]
        car_ref[j] = buf_ref[tm:tm + 8, :]
        st_ref[...] = buf_ref[tm + 8 - (kw - 1):tm + 8, :]
        return y

    gate = conv(wg_ref, cg_ref, bg_ref, bufg_ref, carg_ref, sg_ref)
    up = conv(wu_ref, cu_ref, bu_ref, bufu_ref, caru_ref, su_ref)
    act = (_gelu_tanh(gate) * up).astype(bf16)
    y_ref[...] += _dot(act, wd_ref[...])


def _ffn_prompt(x, g, w_up, w_conv, b_conv, w_down, tm, tf):
    bsz, t, d = x.shape
    nj = D_FF // tf
    kw = CONV_FFN
    gcol = lambda shape: pl.BlockSpec(shape, lambda b, ti, j: (0, j))
    ucol = lambda shape: pl.BlockSpec(shape, lambda b, ti, j: (0, nj + j))
    y, sg, su = pl.pallas_call(
        functools.partial(_ffn_prompt_kernel, tm=tm),
        grid=(bsz, t // tm, nj),
        in_specs=[pl.BlockSpec((None, tm, d), lambda b, ti, j: (b, ti, 0)),
                  pl.BlockSpec((1, d), lambda b, ti, j: (0, 0)),
                  gcol((d, tf)), ucol((d, tf)), gcol((kw, tf)), ucol((kw, tf)), gcol((1, tf)), ucol((1, tf)),
                  pl.BlockSpec((tf, d), lambda b, ti, j: (j, 0))],
        out_specs=[pl.BlockSpec((None, tm, d), lambda b, ti, j: (b, ti, 0)),
                   pl.BlockSpec((None, None, kw - 1, tf), lambda b, ti, j: (b, ti, 0, j)),
                   pl.BlockSpec((None, None, kw - 1, tf), lambda b, ti, j: (b, ti, 0, j))],
        out_shape=[jax.ShapeDtypeStruct((bsz, t, d), f32),
                   jax.ShapeDtypeStruct((bsz, t // tm, kw - 1, D_FF), f32),
                   jax.ShapeDtypeStruct((bsz, t // tm, kw - 1, D_FF), f32)],
        scratch_shapes=[pltpu.VMEM((tm, d), bf16),
                        pltpu.VMEM((tm + 8, tf), f32), pltpu.VMEM((tm + 8, tf), f32),
                        pltpu.VMEM((nj, 8, tf), f32), pltpu.VMEM((nj, 8, tf), f32)],
        compiler_params=_cparams(3),
        name="conv_ffn_prompt",
    )(x, g, w_up, w_up, w_conv, w_conv, b_conv, b_conv, w_down)
    return y, jnp.concatenate([sg[:, -1], su[:, -1]], axis=-1)


def _ffn_sample_kernel(x_ref, g_ref, wg_ref, wu_ref, cg_ref, cu_ref, bg_ref, bu_ref, wd_ref,
                       pg0_ref, pg1_ref, pu0_ref, pu1_ref, y_ref, ug_ref, uu_ref, h_ref):
    @pl.when(pl.program_id(0) == 0)
    def _():
        x = x_ref[...]
        ms = jnp.mean(x * x, axis=-1, keepdims=True)
        h_ref[...] = (x * lax.rsqrt(ms + EPS) * g_ref[...]).astype(bf16)
        y_ref[...] = x

    def conv(w_ref, cw_ref, b_ref, p0_ref, p1_ref, u_ref):
        u = _dot(h_ref[...], w_ref[...])
        u_ref[...] = u
        return b_ref[...] + p0_ref[...] * cw_ref[0:1, :] + p1_ref[...] * cw_ref[1:2, :] + u * cw_ref[2:3, :]

    gate = conv(wg_ref, cg_ref, bg_ref, pg0_ref, pg1_ref, ug_ref)
    up = conv(wu_ref, cu_ref, bu_ref, pu0_ref, pu1_ref, uu_ref)
    y_ref[...] += _dot((_gelu_tanh(gate) * up).astype(bf16), wd_ref[...])


def _ffn_sample(x, g, w_up, w_conv, b_conv, w_down, state, tf):
    s, d = x.shape
    nj = D_FF // tf
    kw = CONV_FFN
    assert kw == 3
    col = lambda shape, off: pl.BlockSpec(shape, lambda j: (0, off + j))
    y, ug, uu = pl.pallas_call(
        _ffn_sample_kernel,
        grid=(nj,),
        in_specs=[pl.BlockSpec((s, d), lambda j: (0, 0)), pl.BlockSpec((1, d), lambda j: (0, 0)),
                  col((d, tf), 0), col((d, tf), nj), col((kw, tf), 0), col((kw, tf), nj),
                  col((1, tf), 0), col((1, tf), nj),
                  pl.BlockSpec((tf, d), lambda j: (j, 0)),
                  col((s, tf), 0), col((s, tf), 2 * nj), col((s, tf), nj), col((s, tf), 3 * nj)],
        out_specs=[pl.BlockSpec((s, d), lambda j: (0, 0)), col((s, tf), 0), col((s, tf), 0)],
        out_shape=[jax.ShapeDtypeStruct((s, d), f32),
                   jax.ShapeDtypeStruct((s, D_FF), f32), jax.ShapeDtypeStruct((s, D_FF), f32)],
        scratch_shapes=[pltpu.VMEM((s, d), bf16)],
        compiler_params=_cparams(1),
        name="conv_ffn_sample",
    )(x, g, w_up, w_up, w_conv, w_conv, b_conv, b_conv, w_down, state, state, state, state)
    return y, jnp.concatenate([ug, uu], axis=-1)


def _final_norm_kernel(x_ref, g_ref, y_ref):
    x = x_ref[...]
    ms = jnp.mean(x * x, axis=-1, keepdims=True)
    y_ref[...] = x * lax.rsqrt(ms + EPS) * g_ref[...]


def _final_norm(x, g, tm):
    m, d = x.shape
    return pl.pallas_call(
        _final_norm_kernel,
        grid=(m // tm,),
        in_specs=[pl.BlockSpec((tm, d), lambda i: (i, 0)), pl.BlockSpec((1, d), lambda i: (0, 0))],
        out_specs=pl.BlockSpec((tm, d), lambda i: (i, 0)),
        out_shape=jax.ShapeDtypeStruct((m, d), f32),
        compiler_params=_cparams(1),
        name="final_norm",
    )(x, g)


def _pack_w_in(w_in):
    splits = np.cumsum([WA, WA, WA, CB_CH, WB_V, H_B, H_B, WC_Q, WC_KV, WC_KV])
    qa, ka, va, cb, zb, adec, bgate, qc, kc, vc, gates = jnp.split(w_in, splits.tolist(), axis=-1)
    lead = w_in.shape[:-1]
    qc = qc.reshape(lead + (H_C, HD_C))[..., list(C_HEAD_ORDER), :].reshape(lead + (WC_Q,))
    ab = jnp.concatenate([adec, bgate, jnp.zeros(lead + (LANE - 2 * H_B,), w_in.dtype)], axis=-1)
    pad = jnp.zeros(lead + (LANE,), w_in.dtype)
    packed = jnp.concatenate([gates, qa, ka, va, cb, zb, qc, ab, kc, vc, pad], axis=-1)
    assert packed.shape[-1] == N_COLS
    return packed.astype(bf16)


def _pick(n, pref):
    for t in pref:
        if n % t == 0:
            return t
    return n


def kernel(x_prompt, x_sample, cache_a_k, cache_a_v, cache_c_k, cache_c_v, state_b_conv, state_b_ssm, state_ffn_conv, norm_mix, w_in, w_conv_b, a_log, dt_bias, gdn_norm, sinks, w_br_a, w_br_b, w_br_c, w_out, norm_ffn, w_up, w_conv_ffn, b_conv_ffn, w_down, norm_final):
    depth = w_in.shape[0]
    bp, t, d = x_prompt.shape
    ns = x_sample.shape[0]
    assert x_sample.shape[1] == 1 and d == D_MODEL and t % 256 == 0

    slopes_a, slopes_c = _alibi_slopes()
    order = list(C_HEAD_ORDER)
    slopes_c_col = jnp.asarray(slopes_c[order]).reshape(H_C, 1)

    w_in_p = _pack_w_in(w_in)
    w_br_a16 = w_br_a.astype(bf16)
    w_br_b16 = w_br_b.astype(bf16)
    w_br_c16 = w_br_c.reshape(depth, H_C, HD_C, d)[:, order].reshape(depth, WC_Q, d).astype(bf16)
    w_out16 = w_out.astype(bf16)
    w_up16 = w_up.astype(bf16)
    w_down16 = w_down.astype(bf16)
    gate_params = jnp.zeros((depth, 8, LANE), f32)
    gate_params = gate_params.at[:, 0, :H_B].set(a_log.astype(f32)).at[:, 1, :H_B].set(dt_bias.astype(f32))

    tm_p = _pick(bp * t, (2048, 1024, 512, 256))
    tm_ffn = _pick(t, (1024, 512, 256))
    tn = _pick(N_COLS, (512,))
    tf = 512
    bs = _pick(ns, (8,))
    cache_a_kt = jnp.transpose(cache_a_k, (0, 1, 3, 4, 2))
    cache_a_vt = jnp.transpose(cache_a_v, (0, 1, 3, 4, 2))

    xp = x_prompt.reshape(bp * t, d)
    xs = x_sample.reshape(ns, d)
    outs = {k: [] for k in ("akp", "avp", "ckp", "cvp", "bcp", "bsp", "fcp", "aks", "avs", "cks", "cvs", "bcs",
                            "bss", "fcs")}
    ra = min(WIN_A, t)
    rc = min(WIN_C, t)
    for l in range(depth):
        g_mix = norm_mix[l].reshape(1, d)
        sinks_l = sinks[l].astype(f32)
        proj = _norm_matmul(xp, g_mix, w_in_p[l], tm_p, tn)
        proj3 = proj.reshape(bp, t, N_COLS)
        far_w, far_d = A_PATTERNS[-1]
        if (t // far_d) % LANE == 0:
            far = _band_attn(proj3, jnp.asarray(slopes_a), sinks_l, patterns=A_PATTERNS[-1:], stride=far_d,
                             tq=min(t // far_d, 256), stats_out=True, name="band_attn_a_far")
            o_a = _band_attn(proj3, jnp.asarray(slopes_a), sinks_l, patterns=A_PATTERNS[:-1], tq=256, partial=far,
                             name="band_attn_a_near")
        else:
            o_a = _band_attn(proj3, jnp.asarray(slopes_a), sinks_l, patterns=A_PATTERNS, tq=256, name="band_attn_a")
        o_c = _band_attn(proj3, jnp.asarray(slopes_c), sinks_l, patterns=((WIN_C - 1, 1),), tq=128, mixer_c=True,
                         name="band_attn_c")
        o_b, ssm_p = _gdn_prompt(proj3, w_conv_b[l], gate_params[l], gdn_norm[l].reshape(1, DV_B), c=128)
        xp = _merge(xp, proj, o_a.reshape(bp * t, WA), o_b.reshape(bp * t, WB_V), o_c.reshape(bp * t, WC_Q),
                    w_br_a16[l], w_br_b16[l], w_br_c16[l], w_out16[l], 256)
        y3, fc_p = _ffn_prompt(xp.reshape(bp, t, d), norm_ffn[l].reshape(1, d), w_up16[l], w_conv_ffn[l],
                               b_conv_ffn[l].reshape(1, 2 * D_FF), w_down16[l], tm_ffn, tf)
        xp = y3.reshape(bp * t, d)
        ka = proj3[:, t - ra:, COL_KA * LANE:COL_KA * LANE + WA]
        va = proj3[:, t - ra:, COL_VA * LANE:COL_VA * LANE + WA]
        outs["akp"].append(ka.reshape(bp, ra, H_A, HD_A))
        outs["avp"].append(va.reshape(bp, ra, H_A, HD_A))
        outs["ckp"].append(proj3[:, t - rc:, COL_KC * LANE:(COL_KC + 1) * LANE].reshape(bp, rc, KVH_C, HD_C))
        outs["cvp"].append(proj3[:, t - rc:, COL_VC * LANE:(COL_VC + 1) * LANE].reshape(bp, rc, KVH_C, HD_C))
        outs["bcp"].append(proj3[:, t - (CONV_B - 1):, COL_CB * LANE:COL_CB * LANE + CB_CH])
        outs["bsp"].append(ssm_p)
        outs["fcp"].append(fc_p)
        proj_s = _norm_matmul(xs, g_mix, w_in_p[l], ns, tn)
        heads_a = lambda col: proj_s[:, col * LANE:col * LANE + WA].reshape(ns, H_A, HD_A)
        ka_s, va_s = heads_a(COL_KA), heads_a(COL_VA)
        lanes = lambda a: jnp.broadcast_to(a[..., None], a.shape + (LANE,))
        o_a = _attn_a_sample(l, lanes(heads_a(COL_QA)), lanes(ka_s), lanes(va_s), cache_a_kt, cache_a_vt,
                             slopes_a)[..., 0].reshape(ns, WA)
        pc = cache_c_k.shape[2]
        o_c = _attn_c_sample(proj_s, cache_c_k[l].reshape(ns, pc, WC_KV), cache_c_v[l].reshape(ns, pc, WC_KV),
                             slopes_c_col, sinks_l[jnp.asarray(order)].reshape(H_C, 1), bs)
        conv_st = state_b_conv[l]
        o_b, ssm_s = _gdn_sample(l, proj_s, conv_st.reshape(ns, (CONV_B - 1) * CB_CH), state_b_ssm, w_conv_b[l],
                                 gate_params[l], gdn_norm[l].reshape(1, DV_B), bs)
        xs = _merge(xs, proj_s, o_a, o_b, o_c, w_br_a16[l], w_br_b16[l], w_br_c16[l], w_out16[l], ns)
        ffn_st = state_ffn_conv[l]
        xs, u_s = _ffn_sample(xs, norm_ffn[l].reshape(1, d), w_up16[l], w_conv_ffn[l],
                              b_conv_ffn[l].reshape(1, 2 * D_FF), w_down16[l],
                              ffn_st.reshape(ns, (CONV_FFN - 1) * 2 * D_FF), tf)
        outs["aks"].append(ka_s[:, None])
        outs["avs"].append(va_s[:, None])
        outs["cks"].append(proj_s[:, COL_KC * LANE:(COL_KC + 1) * LANE].reshape(ns, 1, KVH_C, HD_C))
        outs["cvs"].append(proj_s[:, COL_VC * LANE:(COL_VC + 1) * LANE].reshape(ns, 1, KVH_C, HD_C))
        xb_new = proj_s[:, COL_CB * LANE:COL_CB * LANE + CB_CH]
        outs["bcs"].append(jnp.concatenate([conv_st[:, 1:], xb_new[:, None, :]], axis=1))
        outs["bss"].append(ssm_s)
        outs["fcs"].append(jnp.concatenate([ffn_st[:, 1:], u_s[:, None, :]], axis=1))

    g_fin = norm_final.reshape(1, d)
    y_prompt = _final_norm(xp, g_fin, tm_p).reshape(bp, t, d)
    y_sample = _final_norm(xs, g_fin, ns).reshape(ns, 1, d)
    st = {k: jnp.stack(v) for k, v in outs.items()}
    return (y_prompt, y_sample, st["akp"], st["avp"], st["ckp"], st["cvp"], st["bcp"], st["bsp"], st["fcp"],
            st["aks"], st["avs"], st["cks"], st["cvs"], st["bcs"], st["bss"], st["fcs"])
```

```python
import functools
import math

import numpy as np
import jax
import jax.numpy as jnp
from jax import lax
from jax.experimental import pallas as pl
from jax.experimental.pallas import tpu as pltpu

D_MODEL = 1024
H_A, HD_A = 8, 64
A_PATTERNS = ((128, 1), (512, 4), (2048, 16))
WIN_A = 2048
H_B, DK_B, DV_B = 4, 128, 128
CONV_B = 4
CHUNK_B = 64
H_C, KVH_C, HD_C = 8, 2, 64
WIN_C = 128
D_FF = 3 * D_MODEL
CONV_FFN = 3
EPS = 1e-6
NEG = -1e30
LOG2E = math.log2(math.e)

WA = H_A * HD_A
WB_QK = H_B * DK_B
WB_V = H_B * DV_B
WC_Q = H_C * HD_C
WC_KV = KVH_C * HD_C
CB_CH = 2 * WB_QK + WB_V

LANE = 128
C_HEAD_ORDER = (0, 4, 1, 5, 2, 6, 3, 7)

COL_GATES = 0
COL_QA, COL_KA, COL_VA = 24, 28, 32
COL_CB = 36
COL_ZB = 48
COL_QC = 52
COL_AB = 56
COL_KC, COL_VC = 57, 58
N_COLS = 60 * LANE

VMEM_LIMIT = 53040 * 1024

f32 = jnp.float32
bf16 = jnp.bfloat16


def _cparams(n_axes):
    return pltpu.CompilerParams(dimension_semantics=("arbitrary",) * n_axes, vmem_limit_bytes=VMEM_LIMIT)


def _alibi_slopes():
    n = H_A + H_C
    s = (2.0 ** (-8.0 * (np.arange(n) + 1) / n)).astype(np.float32)
    return s[H_C:], s[:H_C]


def _sigmoid(x):
    return 1.0 / (1.0 + jnp.exp(-x))


def _silu(x):
    return x * _sigmoid(x)


def _softplus(x):
    return jnp.maximum(x, 0.0) + jnp.log1p(jnp.exp(-jnp.abs(x)))


def _gelu_tanh(x):
    c = math.sqrt(2.0 / math.pi)
    return 0.5 * x * (1.0 + jnp.tanh(c * (x + 0.044715 * (x * x * x))))


def _dot(a, b):
    return jnp.dot(a, b, preferred_element_type=f32)


def _dot_nt(a, b):
    return lax.dot_general(a, b, (((1,), (1,)), ((), ())), preferred_element_type=f32)


def _dot_tn(a, b):
    return lax.dot_general(a, b, (((0,), (0,)), ((), ())), preferred_element_type=f32)


def _dot_f32(a, b):
    return jnp.dot(a, b, preferred_element_type=f32, precision=lax.Precision.HIGHEST)


def _norm_matmul_kernel(x_ref, g_ref, w_ref, o_ref, h_ref):
    @pl.when(pl.program_id(1) == 0)
    def _():
        x = x_ref[...]
        ms = jnp.mean(x * x, axis=-1, keepdims=True)
        h_ref[...] = (x * lax.rsqrt(ms + EPS) * g_ref[...]).astype(bf16)

    o_ref[...] = _dot(h_ref[...], w_ref[...])


def _norm_matmul(x, g, w, tm, tn):
    m, d = x.shape
    n = w.shape[1]
    return pl.pallas_call(
        _norm_matmul_kernel,
        grid=(m // tm, n // tn),
        in_specs=[pl.BlockSpec((tm, d), lambda i, j: (i, 0)),
                  pl.BlockSpec((1, d), lambda i, j: (0, 0)),
                  pl.BlockSpec((d, tn), lambda i, j: (0, j))],
        out_specs=pl.BlockSpec((tm, tn), lambda i, j: (i, j)),
        out_shape=jax.ShapeDtypeStruct((m, n), f32),
        scratch_shapes=[pltpu.VMEM((tm, d), bf16)],
        compiler_params=_cparams(2),
        name="norm_in_proj",
    )(x, g, w)


def _band_bias(patterns, stride, dist, slope):
    tok = dist * stride
    mult = jnp.zeros(dist.shape, f32)
    for w, d in patterns:
        hit = (tok <= w) & ((tok & (d - 1)) == 0)
        mult = mult + jnp.where(hit, 1.0, 0.0)
    logm = jnp.where(mult > 2.5, math.log(3.0), jnp.where(mult > 1.5, math.log(2.0), 0.0))
    return jnp.where((dist >= 0) & (mult > 0.5), logm - slope * tok.astype(f32), NEG)


def _band_attn_kernel(slope_ref, sink_ref, q_ref, k_ref, v_ref, *rest,
                      tq, nd, nt, head_a, head_t, head_b, patterns, stride, use_sink, scale, variants,
                      merge_in, stats_out, fold_q=False):
    if merge_in:
        acc_in_ref, m_in_ref, l_in_ref, *rest = rest
    if stats_out:
        o_ref, m_out_ref, l_out_ref, *rest = rest
    else:
        o_ref, *rest = rest
    use16 = stride == 1 and not fold_q
    if use16:
        bias_ref, s_ref, p_ref, k16_ref, v16_ref = rest
    else:
        bias_ref, s_ref, p_ref = rest
    hp = pl.program_id(0)
    b = pl.program_id(1)
    qi = pl.program_id(2)
    ng = 2 * nt
    strip = 64
    half = ng * tq // 2
    head = lambda tt, hh: hp * head_a + tt * head_t + hh * head_b

    @pl.when((b == 0) & (qi == 0))
    def _():
        row = lax.broadcasted_iota(jnp.int32, (tq, tq), 0)
        col = lax.broadcasted_iota(jnp.int32, (tq, tq), 1)
        for tt in range(nt):
            for hh in range(2):
                rows = slice((2 * tt + hh) * tq, (2 * tt + hh + 1) * tq)
                slope = slope_ref[head(tt, hh)]
                for d in range(nd):
                    bias = _band_bias(patterns, stride, d * tq + row - col, slope)
                    bias_ref[d, rows, :] = jnp.where(bias > 0.5 * NEG, bias * LOG2E, NEG)
        bias_ref[nd] = jnp.full((ng * tq, tq), NEG, f32)

    if use16:
        @pl.when(qi == 0)
        def _():
            k16_ref[...] = k_ref[...].astype(bf16)
            v16_ref[...] = v_ref[...].astype(bf16)

    lane = lax.broadcasted_iota(jnp.int32, (tq, LANE), 1)
    lo = lane < LANE // 2
    rows_of = lambda cls: slice(None) if stride == 1 else pl.ds(cls, tq, stride=stride)

    def attend(n, cls=0, qv=None, qrows=None):
        qv = qi if qv is None else qv
        static_q = isinstance(qv, int)
        qrows = rows_of(cls) if qrows is None else qrows
        blocks = []
        for tt in range(nt):
            q = q_ref[qrows, tt * LANE:(tt + 1) * LANE] * (scale * LOG2E)
            blocks += [jnp.where(lo, q, 0.0), jnp.where(lo, 0.0, q)]
        qs = jnp.concatenate(blocks, axis=0).astype(bf16)
        w0 = max(qv - (n - 1), 0) if static_q else jnp.maximum(qv - (n - 1), 0)
        for c in range(n):
            delta = qv - (w0 + c)
            if static_q:
                bidx = delta if 0 <= delta < nd else nd
            else:
                bidx = jnp.where((delta >= 0) & (delta < nd), delta, nd)
            if stride == 1:
                kstart = (w0 + c) * tq if static_q else pl.multiple_of((w0 + c) * tq, tq)
                kt = k16_ref[pl.ds(kstart, tq), :] if use16 else k_ref[pl.ds(kstart, tq), :].astype(bf16)
            else:
                kt = k_ref[rows_of(cls), :].astype(bf16)
            for hrows in (slice(0, half), slice(half, 2 * half)):
                s_ref[hrows, c * tq:(c + 1) * tq] = _dot_nt(qs[hrows], kt) + bias_ref[bidx, hrows, :]
        nblk = n * tq // LANE
        ls, ms, sinks = [], [], []
        for r in range(ng * tq // strip):
            rows = slice(r * strip, (r + 1) * strip)
            mx = s_ref[rows, 0:LANE]
            for j in range(1, nblk):
                mx = jnp.maximum(mx, s_ref[rows, j * LANE:(j + 1) * LANE])
            m = jnp.broadcast_to(jnp.max(mx, axis=-1, keepdims=True), (strip, LANE))
            if use_sink:
                g = r * strip // tq
                sink = jnp.full((strip, LANE), sink_ref[head(g // 2, g % 2)], f32) * LOG2E
                m = jnp.maximum(m, sink)
                sinks.append(sink)
            ms.append(m)
        for r in range(ng * tq // strip):
            rows = slice(r * strip, (r + 1) * strip)
            m = ms[r]
            lsum = jnp.zeros((strip, LANE), f32)
            for j in range(nblk):
                p = jnp.exp2(s_ref[rows, j * LANE:(j + 1) * LANE] - m)
                lsum = lsum + p
                p_ref[rows, j * LANE:(j + 1) * LANE] = p.astype(bf16)
            l = jnp.broadcast_to(jnp.sum(lsum, axis=-1, keepdims=True), (strip, LANE))
            if use_sink:
                l = l + jnp.exp2(sinks[r] - m)
            ls.append(l)
        if stride == 1:
            vstart = w0 * tq if static_q else pl.multiple_of(w0 * tq, tq)
            vt = v16_ref[pl.ds(vstart, n * tq), :] if use16 else v_ref[pl.ds(vstart, n * tq), :].astype(bf16)
        else:
            vt = v_ref[rows_of(cls), :].astype(bf16)
        acc = jnp.concatenate([_dot(p_ref[0:half, 0:n * tq], vt), _dot(p_ref[half:2 * half, 0:n * tq], vt)],
                              axis=0)
        l = jnp.concatenate(ls, axis=0)
        m = jnp.concatenate(ms, axis=0)
        halves = lambda a, tt: jnp.where(lo, a[2 * tt * tq:(2 * tt + 1) * tq], a[(2 * tt + 1) * tq:(2 * tt + 2) * tq])
        for tt in range(nt):
            cols = slice(tt * LANE, (tt + 1) * LANE)
            acc_t = halves(acc, tt)
            m_t = halves(m, tt)
            l_t = halves(l, tt)
            if merge_in:
                m_in = m_in_ref[qrows, cols]
                m_tot = jnp.maximum(m_t, m_in)
                a, a_in = jnp.exp2(m_t - m_tot), jnp.exp2(m_in - m_tot)
                acc_t = acc_t * a + acc_in_ref[qrows, cols] * a_in
                l_t = l_t * a + l_in_ref[qrows, cols] * a_in
                m_t = m_tot
            if stats_out:
                o_ref[qrows, cols] = acc_t
                m_out_ref[qrows, cols] = m_t
                l_out_ref[qrows, cols] = l_t
            else:
                o_ref[qrows, cols] = acc_t / l_t

    if stride > 1:
        for cls in range(stride):
            attend(1, cls)
        return
    if fold_q:
        for qs in range(q_ref.shape[0] // tq):
            attend(min(qs, nd - 1) + 1, 0, qs, slice(qs * tq, (qs + 1) * tq))
        return
    prev = 0
    for n in variants:
        cond = (qi >= prev) if n == variants[-1] else ((qi >= prev) & (qi < n))
        pl.when(cond)(functools.partial(attend, n))
        prev = n


def _band_attn(proj, slopes, sinks, *, patterns, tq, stride=1, partial=None, stats_out=False, mixer_c=False,
               fold_q=False, name):
    bsz, t, ncols = proj.shape
    length = t // stride
    assert stride == 1 or (length == tq and partial is None)
    if mixer_c:
        nhp, nt = 1, 4
        head_a, head_t, head_b = 0, 1, 4
        qcol, kcol, vcol = COL_QC // 4, COL_KC, COL_VC
        hp_k, hd = 0, HD_C
    else:
        nhp, nt = 4, 1
        head_a, head_t, head_b = 2, 0, 1
        qcol, kcol, vcol = COL_QA, COL_KA, COL_VA
        hp_k, hd = 1, HD_A
    span = max(w for w, _ in patterns) // stride
    nd = min(-(-span // tq) + 1, length // tq)
    variants = tuple(sorted({-(-nd // 3), -(-2 * nd // 3), nd}))
    kern = functools.partial(_band_attn_kernel, tq=tq, nd=nd, nt=nt, head_a=head_a, head_t=head_t, head_b=head_b,
                             patterns=patterns, stride=stride, use_sink=mixer_c, scale=hd ** -0.5,
                             variants=variants, merge_in=partial is not None, stats_out=stats_out, fold_q=fold_q)
    rows = t if fold_q else tq * stride
    smem = pl.BlockSpec(memory_space=pltpu.SMEM)
    in_specs = [smem, smem,
                pl.BlockSpec((None, rows, nt * LANE), lambda hp, b, qi: (b, qi, qcol + hp)),
                pl.BlockSpec((None, t, LANE), lambda hp, b, qi: (b, 0, kcol + hp * hp_k)),
                pl.BlockSpec((None, t, LANE), lambda hp, b, qi: (b, 0, vcol + hp * hp_k))]
    operands = [slopes, sinks, proj, proj, proj]
    tile = pl.BlockSpec((None, rows, nt * LANE), lambda hp, b, qi: (b, qi, hp))
    if partial is not None:
        in_specs += [tile] * 3
        operands += list(partial)
    scratch = [pltpu.VMEM((nd + 1, 2 * nt * tq, tq), f32), pltpu.VMEM((2 * nt * tq, nd * tq), f32),
               pltpu.VMEM((2 * nt * tq, nd * tq), bf16)]
    if stride == 1 and not fold_q:
        scratch += [pltpu.VMEM((t, LANE), bf16), pltpu.VMEM((t, LANE), bf16)]
    n_out = 3 if stats_out else 1
    outs = pl.pallas_call(
        kern,
        grid=(nhp, bsz, t // rows),
        in_specs=in_specs,
        out_specs=[tile] * n_out,
        out_shape=[jax.ShapeDtypeStruct((bsz, t, 4 * LANE), f32)] * n_out,
        scratch_shapes=scratch,
        compiler_params=_cparams(3),
        name=name,
    )(*operands)
    return outs if stats_out else outs[0]


def _attn_a_sample_kernel(bias_ref, q_ref, kn_ref, vn_ref, kt_ref, vt_ref, o_ref, *, scale):
    nh, hd, p = kt_ref.shape
    for h in range(nh):
        qc = q_ref[h] * scale
        s_new = jnp.sum(qc * kn_ref[h], axis=0, keepdims=True)[:, 0:1]
        s = jnp.concatenate(
            [jnp.sum(kt_ref[h, :, j * LANE:(j + 1) * LANE] * qc, axis=0, keepdims=True) for j in range(p // LANE)],
            axis=1) + bias_ref[h:h + 1, :]
        m = jnp.maximum(jnp.max(s, axis=-1, keepdims=True), s_new)
        pr = jnp.exp(s - m)
        p_new = float(len(A_PATTERNS)) * jnp.exp(s_new - m)
        l = jnp.sum(pr, axis=-1, keepdims=True) + p_new
        acc = vt_ref[h, :, 0:LANE] * pr[:, 0:LANE]
        for j in range(1, p // LANE):
            acc = acc + vt_ref[h, :, j * LANE:(j + 1) * LANE] * pr[:, j * LANE:(j + 1) * LANE]
        o = (jnp.sum(acc, axis=-1, keepdims=True) + p_new * vn_ref[h]) / l
        o_ref[h] = o


def _attn_a_sample(layer, q, kn, vn, cache_kt, cache_vt, slopes):
    _, s, h, hd, p = cache_kt.shape
    dist = (p - np.arange(p)).astype(np.int64)
    mult = sum(((dist <= wd) & (dist % dl == 0)).astype(np.float64) for wd, dl in A_PATTERNS)
    slopes = np.asarray(slopes, np.float32)
    bias = np.where(mult[None] > 0, np.log(np.maximum(mult, 1.0))[None].astype(np.float32)
                    - slopes[:, None] * dist[None].astype(np.float32), np.float32(NEG)).astype(np.float32)
    tok = pl.BlockSpec((None, h, hd, LANE), lambda i: (i, 0, 0, 0))
    cache = pl.BlockSpec((None, None, h, hd, p), lambda i: (layer, i, 0, 0, 0))
    return pl.pallas_call(
        functools.partial(_attn_a_sample_kernel, scale=hd ** -0.5),
        grid=(s,),
        in_specs=[pl.BlockSpec(bias.shape, lambda i: (0, 0)), tok, tok, tok, cache, cache],
        out_specs=tok,
        out_shape=jax.ShapeDtypeStruct((s, h, hd, LANE), f32),
        compiler_params=_cparams(1),
        name="attn_a_sample",
    )(jnp.asarray(bias), q, kn, vn, cache_kt, cache_vt)


def _attn_c_sample_kernel(slope_ref, sink_ref, q_ref, kn_ref, vn_ref, k_ref, v_ref, o_ref, *, bs, scale):
    rows = k_ref.shape[1]
    prow = lax.broadcasted_iota(jnp.int32, (H_C, LANE), 0)
    plane = lax.broadcasted_iota(jnp.int32, (H_C, LANE), 1)
    halfmask = (plane // HD_C) == (prow % 2)
    j = lax.broadcasted_iota(jnp.int32, (H_C, rows), 1)
    dist = rows - j
    slope = slope_ref[...]
    sink = sink_ref[...]
    bias = jnp.where(dist < WIN_C, -slope * dist.astype(f32), NEG)
    for b in range(bs):
        tiles = []
        for t in range(WC_Q // LANE):
            qt = q_ref[b:b + 1, t * LANE:(t + 1) * LANE] * scale
            tiles += [qt, qt]
        qblk = jnp.where(halfmask, jnp.concatenate(tiles, axis=0), 0.0)
        kn = kn_ref[b:b + 1, :]
        vn = vn_ref[b:b + 1, :]
        s_new = jnp.sum(qblk * kn, axis=-1, keepdims=True)
        s = _dot_nt(qblk.astype(bf16), k_ref[b].astype(bf16)) + bias
        m = jnp.maximum(jnp.maximum(s_new, jnp.max(s, axis=-1, keepdims=True)), sink)
        p = jnp.exp(s - m)
        p_new = jnp.exp(s_new - m)
        l = jnp.sum(p, axis=-1, keepdims=True) + p_new + jnp.exp(sink - m)
        acc = _dot(p.astype(bf16), v_ref[b].astype(bf16)) + p_new * vn
        o = jnp.where(halfmask, acc / l, 0.0)
        for t in range(WC_Q // LANE):
            o_ref[b:b + 1, t * LANE:(t + 1) * LANE] = o[2 * t:2 * t + 1, :] + o[2 * t + 1:2 * t + 2, :]


def _attn_c_sample(proj_s, cache_k, cache_v, slopes, sinks, bs):
    s, p, w = cache_k.shape
    assert p == WIN_C and w == LANE
    col = pl.BlockSpec((H_C, 1), lambda i: (0, 0))
    cache = pl.BlockSpec((bs, p, w), lambda i: (i, 0, 0))
    return pl.pallas_call(
        functools.partial(_attn_c_sample_kernel, bs=bs, scale=HD_C ** -0.5),
        grid=(s // bs,),
        in_specs=[col, col,
                  pl.BlockSpec((bs, WC_Q), lambda i: (i, COL_QC * LANE // WC_Q)),
                  pl.BlockSpec((bs, LANE), lambda i: (i, COL_KC)),
                  pl.BlockSpec((bs, LANE), lambda i: (i, COL_VC)),
                  cache, cache],
        out_specs=pl.BlockSpec((bs, WC_Q), lambda i: (i, 0)),
        out_shape=jax.ShapeDtypeStruct((s, WC_Q), f32),
        compiler_params=_cparams(1),
        name="attn_c_sample",
    )(slopes, sinks, proj_s, proj_s, proj_s, cache_k, cache_v)


def _gdn_gates(ab, gp_ref):
    g = -jnp.exp(gp_ref[0:1, :]) * _softplus(ab + gp_ref[1:2, :])
    return g, _sigmoid(ab)


def _l2norm(x):
    return x * lax.rsqrt(jnp.sum(x * x, axis=-1, keepdims=True) + EPS)


def _gated_out(o, z, gn):
    y = o * lax.rsqrt(jnp.mean(o * o, axis=-1, keepdims=True) + EPS) * gn
    return y * _silu(z)


def _gdn_prompt_kernel(x_ref, z_ref, ab_ref, wconv_ref, gp_ref, gn_ref, o_ref, s_ref, buf_ref, act_ref, *, nb, c):
    ti = pl.program_id(0)
    nch = nb * H_B

    @pl.when(ti == 0)
    def _():
        buf_ref[:, 0:8, :] = jnp.zeros((nb, 8, CB_CH), f32)
        s_ref[...] = jnp.zeros(s_ref.shape, f32)

    for b in range(nb):
        buf_ref[b, 8:8 + c, :] = x_ref[b]
        for cb in range(CB_CH // LANE):
            sl = slice(cb * LANE, (cb + 1) * LANE)
            y = buf_ref[b, pl.ds(8 - (CONV_B - 1), c), sl] * wconv_ref[0:1, sl]
            for jw in range(1, CONV_B):
                y = y + buf_ref[b, pl.ds(8 - (CONV_B - 1) + jw, c), sl] * wconv_ref[jw:jw + 1, sl]
            act_ref[b, :, sl] = _silu(y)
        buf_ref[b, 0:8, :] = buf_ref[b, c:c + 8, :]

    g, beta = _gdn_gates(ab_ref[...].reshape(nb * c, LANE), gp_ref)
    rowc = lax.broadcasted_iota(jnp.int32, (nb * c, LANE), 0) & (c - 1)
    sh = 1
    while sh < c:
        g = g + jnp.where(rowc >= sh, pltpu.roll(g, sh, axis=0), 0.0)
        sh *= 2

    qs, ks, vs, zs, gcols, grows, bcols = [], [], [], [], [], [], []
    for b in range(nb):
        gb = g[b * c:(b + 1) * c]
        gb_t = jnp.transpose(gb)
        for h in range(H_B):
            qs.append(_l2norm(act_ref[b, :, h * DK_B:(h + 1) * DK_B]) * (DK_B ** -0.5))
            ks.append(_l2norm(act_ref[b, :, WB_QK + h * DK_B:WB_QK + (h + 1) * DK_B]))
            vs.append(act_ref[b, :, 2 * WB_QK + h * DV_B:2 * WB_QK + (h + 1) * DV_B])
            zs.append(z_ref[b, :, h * DV_B:(h + 1) * DV_B])
            gcols.append(gb[:, h:h + 1])
            grows.append(gb_t[h:h + 1, :])
            bcols.append(beta[b * c:(b + 1) * c, H_B + h:H_B + h + 1])
    q, k, v, z = jnp.stack(qs), jnp.stack(ks), jnp.stack(vs), jnp.stack(zs)
    gcol, grow, bcol = jnp.stack(gcols), jnp.stack(grows), jnp.stack(bcols)

    ri = lax.broadcasted_iota(jnp.int32, (c, c), 0)
    ci = lax.broadcasted_iota(jnp.int32, (c, c), 1)
    incl = (ci <= ri)[None]
    strict = (ci < ri)[None]
    eye = jnp.where(ci == ri, 1.0, 0.0)[None]
    bmm = lambda a, b_: jnp.einsum("nij,njk->nik", a.astype(bf16), b_.astype(bf16), preferred_element_type=f32)
    bmm_nt = lambda a, b_: jnp.einsum("nik,njk->nij", a.astype(bf16), b_.astype(bf16), preferred_element_type=f32)

    decay = jnp.exp(jnp.where(incl, gcol - grow, NEG))
    eg = jnp.exp(gcol)
    kb = k * bcol
    a_low = jnp.where(strict, bmm_nt(kb, k) * decay, 0.0)
    blk = min(16, c)
    rb, cb_ = ri // blk, ci // blk
    pw = jnp.where((rb == cb_)[None], -a_low, 0.0)
    tmat = eye + pw
    for _ in range(int(math.log2(blk)) - 1):
        pw = bmm(pw, pw)
        tmat = tmat + bmm(tmat, pw)
    while blk < c:
        off = ((ri // (2 * blk)) == (ci // (2 * blk))) & (rb != cb_)
        tmat = tmat - bmm(tmat, bmm(jnp.where(off[None], a_low, 0.0), tmat))
        blk *= 2
        rb, cb_ = ri // blk, ci // blk
    u = bmm(tmat, v * bcol)
    w = bmm(tmat, kb * eg)
    qk = jnp.where(incl, bmm_nt(q, k) * decay, 0.0)
    s = s_ref[...].reshape(nch, DK_B, DV_B)
    v_new = u - bmm(w, s)
    o = bmm(q * eg, s) + bmm(qk, v_new)
    glast = gcol[:, c - 1:c, :]
    k_dec = (k * jnp.exp(glast - gcol)).astype(bf16)
    vb = v_new.astype(bf16)
    upd = jnp.stack([_dot_tn(k_dec[n], vb[n]) for n in range(nch)])
    s_ref[...] = (s * jnp.exp(glast) + upd).reshape(nb, H_B, DK_B, DV_B)
    y = _gated_out(o, z, gn_ref[...][None])
    for b in range(nb):
        for h in range(H_B):
            o_ref[b, :, h * DV_B:(h + 1) * DV_B] = y[b * H_B + h]


def _gdn_prompt(proj, w_conv, gate_params, gdn_norm, c):
    bsz, t, _ = proj.shape
    const = lambda shape: pl.BlockSpec(shape, lambda ti: (0,) * len(shape))
    return pl.pallas_call(
        functools.partial(_gdn_prompt_kernel, nb=bsz, c=c),
        grid=(t // c,),
        in_specs=[pl.BlockSpec((bsz, c, CB_CH), lambda ti: (0, ti, COL_CB * LANE // CB_CH)),
                  pl.BlockSpec((bsz, c, WB_V), lambda ti: (0, ti, COL_ZB * LANE // WB_V)),
                  pl.BlockSpec((bsz, c, LANE), lambda ti: (0, ti, COL_AB)),
                  const((CONV_B, CB_CH)), const((8, LANE)), const((1, DV_B))],
        out_specs=[pl.BlockSpec((bsz, c, WB_V), lambda ti: (0, ti, 0)),
                   pl.BlockSpec((bsz, H_B, DK_B, DV_B), lambda ti: (0, 0, 0, 0))],
        out_shape=[jax.ShapeDtypeStruct((bsz, t, WB_V), f32),
                   jax.ShapeDtypeStruct((bsz, H_B, DK_B, DV_B), f32)],
        scratch_shapes=[pltpu.VMEM((bsz, c + 8, CB_CH), f32), pltpu.VMEM((bsz, c, CB_CH), f32)],
        compiler_params=_cparams(1),
        name="gdn_prompt",
    )(proj, proj, proj, w_conv, gate_params, gdn_norm)


def _gdn_sample_kernel(x_ref, c0_ref, c1_ref, c2_ref, z_ref, ab_ref, s_ref, wconv_ref, gp_ref, gn_ref,
                       o_ref, so_ref, *, bs):
    y = (c0_ref[...] * wconv_ref[0:1, :] + c1_ref[...] * wconv_ref[1:2, :] + c2_ref[...] * wconv_ref[2:3, :]
         + x_ref[...] * wconv_ref[3:4, :])
    act = _silu(y)
    g, beta = _gdn_gates(ab_ref[...], gp_ref)
    eg = jnp.exp(g)
    gn = gn_ref[...]
    rowi = lax.broadcasted_iota(jnp.int32, (bs, DK_B), 0)
    for h in range(H_B):
        q = _l2norm(act[:, h * DK_B:(h + 1) * DK_B]) * (DK_B ** -0.5)
        k = _l2norm(act[:, WB_QK + h * DK_B:WB_QK + (h + 1) * DK_B])
        v = act[:, 2 * WB_QK + h * DV_B:2 * WB_QK + (h + 1) * DV_B]
        egc = eg[:, h:h + 1]
        bc = beta[:, H_B + h:H_B + h + 1]
        qg = q * egc
        ks_rows, qs_rows = [], []
        for b in range(bs):
            lhs = jnp.where(rowi == 0, jnp.broadcast_to(k[b:b + 1], (bs, DK_B)),
                            jnp.where(rowi == 1, jnp.broadcast_to(qg[b:b + 1], (bs, DK_B)), 0.0))
            r = _dot(lhs.astype(bf16), s_ref[b, h].astype(bf16))
            ks_rows.append(r[0:1])
            qs_rows.append(r[1:2])
        ks = jnp.concatenate(ks_rows, axis=0)
        qs = jnp.concatenate(qs_rows, axis=0)
        v_new = bc * v - (bc * egc) * ks
        qk = jnp.sum(q * k, axis=-1, keepdims=True)
        o = qs + qk * v_new
        o_ref[:, h * DV_B:(h + 1) * DV_B] = _gated_out(o, z_ref[:, h * DV_B:(h + 1) * DV_B], gn)
        vb = v_new.astype(bf16)
        for b in range(bs):
            km = jnp.where(rowi == b, k, 0.0).astype(bf16)
            so_ref[b, h] = s_ref[b, h] * egc[b:b + 1, :] + _dot_tn(km, vb)


def _gdn_sample(layer, proj_s, conv_state, ssm_state, w_conv, gate_params, gdn_norm, bs):
    s = proj_s.shape[0]
    const = lambda shape: pl.BlockSpec(shape, lambda i: (0,) * len(shape))
    cst = lambda j: pl.BlockSpec((bs, CB_CH), lambda i: (i, j))
    st = pl.BlockSpec((bs, H_B, DK_B, DV_B), lambda i: (i, 0, 0, 0))
    st_in = pl.BlockSpec((None, bs, H_B, DK_B, DV_B), lambda i: (layer, i, 0, 0, 0))
    return pl.pallas_call(
        functools.partial(_gdn_sample_kernel, bs=bs),
        grid=(s // bs,),
        in_specs=[pl.BlockSpec((bs, CB_CH), lambda i: (i, COL_CB * LANE // CB_CH)),
                  cst(0), cst(1), cst(2),
                  pl.BlockSpec((bs, WB_V), lambda i: (i, COL_ZB * LANE // WB_V)),
                  pl.BlockSpec((bs, LANE), lambda i: (i, COL_AB)),
                  st_in, const((CONV_B, CB_CH)), const((8, LANE)), const((1, DV_B))],
        out_specs=[pl.BlockSpec((bs, WB_V), lambda i: (i, 0)), st],
        out_shape=[jax.ShapeDtypeStruct((s, WB_V), f32),
                   jax.ShapeDtypeStruct(ssm_state.shape[1:], f32)],
        compiler_params=_cparams(1),
        name="gdn_sample",
    )(proj_s, conv_state, conv_state, conv_state, proj_s, proj_s, ssm_state, w_conv, gate_params, gdn_norm)


def _merge_kernel(x_ref, g_ref, oa_ref, ob_ref, oc_ref, wa_ref, wb_ref, wc_ref, wo_ref, y_ref):
    d = D_MODEL
    m = _sigmoid(g_ref[:, 0:d]) * _dot(oa_ref[...].astype(bf16), wa_ref[...])
    m = m + _sigmoid(g_ref[:, d:2 * d]) * _dot(ob_ref[...].astype(bf16), wb_ref[...])
    m = m + _sigmoid(g_ref[:, 2 * d:3 * d]) * _dot(oc_ref[...].astype(bf16), wc_ref[...])
    y_ref[...] = x_ref[...] + _dot(m.astype(bf16), wo_ref[...])


def _merge(x, proj, o_a, o_b, o_c, wa, wb, wc, wo, tm):
    m, d = x.shape
    row = lambda w: pl.BlockSpec((tm, w), lambda i: (i, 0))
    const = lambda a: pl.BlockSpec(a.shape, lambda i: (0, 0))
    return pl.pallas_call(
        _merge_kernel,
        grid=(m // tm,),
        in_specs=[row(d), row(3 * d), row(WA), row(WB_V), row(WC_Q), const(wa), const(wb), const(wc), const(wo)],
        out_specs=row(d),
        out_shape=jax.ShapeDtypeStruct((m, d), f32),
        compiler_params=_cparams(1),
        name="merge_out_proj",
    )(x, proj, o_a, o_b, o_c, wa, wb, wc, wo)


def _ffn_prompt_kernel(x_ref, g_ref, wg_ref, wu_ref, cg_ref, cu_ref, bg_ref, bu_ref, wd_ref,
                       y_ref, sg_ref, su_ref, h_ref, bufg_ref, bufu_ref, carg_ref, caru_ref, *, tm):
    ti = pl.program_id(1)
    j = pl.program_id(2)
    kw = CONV_FFN

    @pl.when(j == 0)
    def _():
        x = x_ref[...]
        ms = jnp.mean(x * x, axis=-1, keepdims=True)
        h_ref[...] = (x * lax.rsqrt(ms + EPS) * g_ref[...]).astype(bf16)
        y_ref[...] = x

    @pl.when(ti == 0)
    def _():
        carg_ref[j] = jnp.zeros(carg_ref.shape[1:], f32)
        caru_ref[j] = jnp.zeros(caru_ref.shape[1:], f32)

    def conv(w_ref, cw_ref, b_ref, buf_ref, car_ref, st_ref):
        buf_ref[0:8, :] = car_ref[j]
        buf_ref[8:8 + tm, :] = _dot(h_ref[...], w_ref[...])
        y = b_ref[...] + buf_ref[pl.ds(8 - (kw - 1), tm), :] * cw_ref[0:1, :]
        for jw in range(1, kw):
            y = y + buf_ref[pl.ds(8 - (kw - 1) + jw, tm), :] * cw_ref[jw:jw + 1, :]
        car_ref[j] = buf_ref[tm:tm + 8, :]
        st_ref[...] = buf_ref[tm + 8 - (kw - 1):tm + 8, :]
        return y

    gate = conv(wg_ref, cg_ref, bg_ref, bufg_ref, carg_ref, sg_ref)
    up = conv(wu_ref, cu_ref, bu_ref, bufu_ref, caru_ref, su_ref)
    act = (_gelu_tanh(gate) * up).astype(bf16)
    y_ref[...] += _dot(act, wd_ref[...])


def _ffn_prompt(x, g, w_up, w_conv, b_conv, w_down, tm, tf):
    bsz, t, d = x.shape
    nj = D_FF // tf
    kw = CONV_FFN
    gcol = lambda shape: pl.BlockSpec(shape, lambda b, ti, j: (0, j))
    ucol = lambda shape: pl.BlockSpec(shape, lambda b, ti, j: (0, nj + j))
    y, sg, su = pl.pallas_call(
        functools.partial(_ffn_prompt_kernel, tm=tm),
        grid=(bsz, t // tm, nj),
        in_specs=[pl.BlockSpec((None, tm, d), lambda b, ti, j: (b, ti, 0)),
                  pl.BlockSpec((1, d), lambda b, ti, j: (0, 0)),
                  gcol((d, tf)), ucol((d, tf)), gcol((kw, tf)), ucol((kw, tf)), gcol((1, tf)), ucol((1, tf)),
                  pl.BlockSpec((tf, d), lambda b, ti, j: (j, 0))],
        out_specs=[pl.BlockSpec((None, tm, d), lambda b, ti, j: (b, ti, 0)),
                   pl.BlockSpec((None, None, kw - 1, tf), lambda b, ti, j: (b, ti, 0, j)),
                   pl.BlockSpec((None, None, kw - 1, tf), lambda b, ti, j: (b, ti, 0, j))],
        out_shape=[jax.ShapeDtypeStruct((bsz, t, d), f32),
                   jax.ShapeDtypeStruct((bsz, t // tm, kw - 1, D_FF), f32),
                   jax.ShapeDtypeStruct((bsz, t // tm, kw - 1, D_FF), f32)],
        scratch_shapes=[pltpu.VMEM((tm, d), bf16),
                        pltpu.VMEM((tm + 8, tf), f32), pltpu.VMEM((tm + 8, tf), f32),
                        pltpu.VMEM((nj, 8, tf), f32), pltpu.VMEM((nj, 8, tf), f32)],
        compiler_params=_cparams(3),
        name="conv_ffn_prompt",
    )(x, g, w_up, w_up, w_conv, w_conv, b_conv, b_conv, w_down)
    return y, jnp.concatenate([sg[:, -1], su[:, -1]], axis=-1)


def _ffn_sample_kernel(x_ref, g_ref, wg_ref, wu_ref, cg_ref, cu_ref, bg_ref, bu_ref, wd_ref,
                       pg0_ref, pg1_ref, pu0_ref, pu1_ref, y_ref, ug_ref, uu_ref, h_ref):
    @pl.when(pl.program_id(0) == 0)
    def _():
        x = x_ref[...]
        ms = jnp.mean(x * x, axis=-1, keepdims=True)
        h_ref[...] = (x * lax.rsqrt(ms + EPS) * g_ref[...]).astype(bf16)
        y_ref[...] = x

    def conv(w_ref, cw_ref, b_ref, p0_ref, p1_ref, u_ref):
        u = _dot(h_ref[...], w_ref[...])
        u_ref[...] = u
        return b_ref[...] + p0_ref[...] * cw_ref[0:1, :] + p1_ref[...] * cw_ref[1:2, :] + u * cw_ref[2:3, :]

    gate = conv(wg_ref, cg_ref, bg_ref, pg0_ref, pg1_ref, ug_ref)
    up = conv(wu_ref, cu_ref, bu_ref, pu0_ref, pu1_ref, uu_ref)
    y_ref[...] += _dot((_gelu_tanh(gate) * up).astype(bf16), wd_ref[...])


def _ffn_sample(x, g, w_up, w_conv, b_conv, w_down, state, tf):
    s, d = x.shape
    nj = D_FF // tf
    kw = CONV_FFN
    assert kw == 3
    col = lambda shape, off: pl.BlockSpec(shape, lambda j: (0, off + j))
    y, ug, uu = pl.pallas_call(
        _ffn_sample_kernel,
        grid=(nj,),
        in_specs=[pl.BlockSpec((s, d), lambda j: (0, 0)), pl.BlockSpec((1, d), lambda j: (0, 0)),
                  col((d, tf), 0), col((d, tf), nj), col((kw, tf), 0), col((kw, tf), nj),
                  col((1, tf), 0), col((1, tf), nj),
                  pl.BlockSpec((tf, d), lambda j: (j, 0)),
                  col((s, tf), 0), col((s, tf), 2 * nj), col((s, tf), nj), col((s, tf), 3 * nj)],
        out_specs=[pl.BlockSpec((s, d), lambda j: (0, 0)), col((s, tf), 0), col((s, tf), 0)],
        out_shape=[jax.ShapeDtypeStruct((s, d), f32),
                   jax.ShapeDtypeStruct((s, D_FF), f32), jax.ShapeDtypeStruct((s, D_FF), f32)],
        scratch_shapes=[pltpu.VMEM((s, d), bf16)],
        compiler_params=_cparams(1),
        name="conv_ffn_sample",
    )(x, g, w_up, w_up, w_conv, w_conv, b_conv, b_conv, w_down, state, state, state, state)
    return y, jnp.concatenate([ug, uu], axis=-1)


def _final_norm_kernel(x_ref, g_ref, y_ref):
    x = x_ref[...]
    ms = jnp.mean(x * x, axis=-1, keepdims=True)
    y_ref[...] = x * lax.rsqrt(ms + EPS) * g_ref[...]


def _final_norm(x, g, tm):
    m, d = x.shape
    return pl.pallas_call(
        _final_norm_kernel,
        grid=(m // tm,),
        in_specs=[pl.BlockSpec((tm, d), lambda i: (i, 0)), pl.BlockSpec((1, d), lambda i: (0, 0))],
        out_specs=pl.BlockSpec((tm, d), lambda i: (i, 0)),
        out_shape=jax.ShapeDtypeStruct((m, d), f32),
        compiler_params=_cparams(1),
        name="final_norm",
    )(x, g)


def _pack_w_in(w_in):
    splits = np.cumsum([WA, WA, WA, CB_CH, WB_V, H_B, H_B, WC_Q, WC_KV, WC_KV])
    qa, ka, va, cb, zb, adec, bgate, qc, kc, vc, gates = jnp.split(w_in, splits.tolist(), axis=-1)
    lead = w_in.shape[:-1]
    qc = qc.reshape(lead + (H_C, HD_C))[..., list(C_HEAD_ORDER), :].reshape(lead + (WC_Q,))
    ab = jnp.concatenate([adec, bgate, jnp.zeros(lead + (LANE - 2 * H_B,), w_in.dtype)], axis=-1)
    pad = jnp.zeros(lead + (LANE,), w_in.dtype)
    packed = jnp.concatenate([gates, qa, ka, va, cb, zb, qc, ab, kc, vc, pad], axis=-1)
    assert packed.shape[-1] == N_COLS
    return packed.astype(bf16)


def _pick(n, pref):
    for t in pref:
        if n % t == 0:
            return t
    return n


def kernel(x_prompt, x_sample, cache_a_k, cache_a_v, cache_c_k, cache_c_v, state_b_conv, state_b_ssm, state_ffn_conv, norm_mix, w_in, w_conv_b, a_log, dt_bias, gdn_norm, sinks, w_br_a, w_br_b, w_br_c, w_out, norm_ffn, w_up, w_conv_ffn, b_conv_ffn, w_down, norm_final):
    depth = w_in.shape[0]
    bp, t, d = x_prompt.shape
    ns = x_sample.shape[0]
    assert x_sample.shape[1] == 1 and d == D_MODEL and t % 256 == 0

    slopes_a, slopes_c = _alibi_slopes()
    order = list(C_HEAD_ORDER)
    slopes_c_col = jnp.asarray(slopes_c[order]).reshape(H_C, 1)

    w_in_p = _pack_w_in(w_in)
    w_br_a16 = w_br_a.astype(bf16)
    w_br_b16 = w_br_b.astype(bf16)
    w_br_c16 = w_br_c.reshape(depth, H_C, HD_C, d)[:, order].reshape(depth, WC_Q, d).astype(bf16)
    w_out16 = w_out.astype(bf16)
    w_up16 = w_up.astype(bf16)
    w_down16 = w_down.astype(bf16)
    gate_params = jnp.zeros((depth, 8, LANE), f32)
    gate_params = gate_params.at[:, 0, :H_B].set(a_log.astype(f32)).at[:, 1, :H_B].set(dt_bias.astype(f32))

    tm_p = _pick(bp * t, (2048, 1024, 512, 256))
    tm_ffn = _pick(t, (1024, 512, 256))
    tn = _pick(N_COLS, (512,))
    tf = 512
    bs = _pick(ns, (8,))
    cache_a_kt = jnp.transpose(cache_a_k, (0, 1, 3, 4, 2))
    cache_a_vt = jnp.transpose(cache_a_v, (0, 1, 3, 4, 2))

    xp = x_prompt.reshape(bp * t, d)
    xs = x_sample.reshape(ns, d)
    outs = {k: [] for k in ("akp", "avp", "ckp", "cvp", "bcp", "bsp", "fcp", "aks", "avs", "cks", "cvs", "bcs",
                            "bss", "fcs")}
    ra = min(WIN_A, t)
    rc = min(WIN_C, t)
    for l in range(depth):
        g_mix = norm_mix[l].reshape(1, d)
        sinks_l = sinks[l].astype(f32)
        proj = _norm_matmul(xp, g_mix, w_in_p[l], tm_p, tn)
        proj3 = proj.reshape(bp, t, N_COLS)
        far_w, far_d = A_PATTERNS[-1]
        if (t // far_d) % LANE == 0:
            far = _band_attn(proj3, jnp.asarray(slopes_a), sinks_l, patterns=A_PATTERNS[-1:], stride=far_d,
                             tq=min(t // far_d, 256), stats_out=True, name="band_attn_a_far")
            o_a = _band_attn(proj3, jnp.asarray(slopes_a), sinks_l, patterns=A_PATTERNS[:-1], tq=256, partial=far,
                             fold_q=True, name="band_attn_a_near")
        else:
            o_a = _band_attn(proj3, jnp.asarray(slopes_a), sinks_l, patterns=A_PATTERNS, tq=256, name="band_attn_a")
        o_c = _band_attn(proj3, jnp.asarray(slopes_c), sinks_l, patterns=((WIN_C - 1, 1),), tq=128, mixer_c=True,
                         name="band_attn_c")
        o_b, ssm_p = _gdn_prompt(proj3, w_conv_b[l], gate_params[l], gdn_norm[l].reshape(1, DV_B), c=128)
        xp = _merge(xp, proj, o_a.reshape(bp * t, WA), o_b.reshape(bp * t, WB_V), o_c.reshape(bp * t, WC_Q),
                    w_br_a16[l], w_br_b16[l], w_br_c16[l], w_out16[l], 256)
        y3, fc_p = _ffn_prompt(xp.reshape(bp, t, d), norm_ffn[l].reshape(1, d), w_up16[l], w_conv_ffn[l],
                               b_conv_ffn[l].reshape(1, 2 * D_FF), w_down16[l], tm_ffn, tf)
        xp = y3.reshape(bp * t, d)
        ka = proj3[:, t - ra:, COL_KA * LANE:COL_KA * LANE + WA]
        va = proj3[:, t - ra:, COL_VA * LANE:COL_VA * LANE + WA]
        outs["akp"].append(ka.reshape(bp, ra, H_A, HD_A))
        outs["avp"].append(va.reshape(bp, ra, H_A, HD_A))
        outs["ckp"].append(proj3[:, t - rc:, COL_KC * LANE:(COL_KC + 1) * LANE].reshape(bp, rc, KVH_C, HD_C))
        outs["cvp"].append(proj3[:, t - rc:, COL_VC * LANE:(COL_VC + 1) * LANE].reshape(bp, rc, KVH_C, HD_C))
        outs["bcp"].append(proj3[:, t - (CONV_B - 1):, COL_CB * LANE:COL_CB * LANE + CB_CH])
        outs["bsp"].append(ssm_p)
        outs["fcp"].append(fc_p)
        proj_s = _norm_matmul(xs, g_mix, w_in_p[l], ns, tn)
        heads_a = lambda col: proj_s[:, col * LANE:col * LANE + WA].reshape(ns, H_A, HD_A)
        ka_s, va_s = heads_a(COL_KA), heads_a(COL_VA)
        lanes = lambda a: jnp.broadcast_to(a[..., None], a.shape + (LANE,))
        o_a = _attn_a_sample(l, lanes(heads_a(COL_QA)), lanes(ka_s), lanes(va_s), cache_a_kt, cache_a_vt,
                             slopes_a)[..., 0].reshape(ns, WA)
        pc = cache_c_k.shape[2]
        o_c = _attn_c_sample(proj_s, cache_c_k[l].reshape(ns, pc, WC_KV), cache_c_v[l].reshape(ns, pc, WC_KV),
                             slopes_c_col, sinks_l[jnp.asarray(order)].reshape(H_C, 1), bs)
        conv_st = state_b_conv[l]
        o_b, ssm_s = _gdn_sample(l, proj_s, conv_st.reshape(ns, (CONV_B - 1) * CB_CH), state_b_ssm, w_conv_b[l],
                                 gate_params[l], gdn_norm[l].reshape(1, DV_B), bs)
        xs = _merge(xs, proj_s, o_a, o_b, o_c, w_br_a16[l], w_br_b16[l], w_br_c16[l], w_out16[l], ns)
        ffn_st = state_ffn_conv[l]
        xs, u_s = _ffn_sample(xs, norm_ffn[l].reshape(1, d), w_up16[l], w_conv_ffn[l],
                              b_conv_ffn[l].reshape(1, 2 * D_FF), w_down16[l],
                              ffn_st.reshape(ns, (CONV_FFN - 1) * 2 * D_FF), tf)
        outs["aks"].append(ka_s[:, None])
        outs["avs"].append(va_s[:, None])
        outs["cks"].append(proj_s[:, COL_KC * LANE:(COL_KC + 1) * LANE].reshape(ns, 1, KVH_C, HD_C))
        outs["cvs"].append(proj_s[:, COL_VC * LANE:(COL_VC + 1) * LANE].reshape(ns, 1, KVH_C, HD_C))
        xb_new = proj_s[:, COL_CB * LANE:COL_CB * LANE + CB_CH]
        outs["bcs"].append(jnp.concatenate([conv_st[:, 1:], xb_new[:, None, :]], axis=1))
        outs["bss"].append(ssm_s)
        outs["fcs"].append(jnp.concatenate([ffn_st[:, 1:], u_s[:, None, :]], axis=1))

    g_fin = norm_final.reshape(1, d)
    y_prompt = _final_norm(xp, g_fin, tm_p).reshape(bp, t, d)
    y_sample = _final_norm(xs, g_fin, ns).reshape(ns, 1, d)
    st = {k: jnp.stack(v) for k, v in outs.items()}
    return (y_prompt, y_sample, st["akp"], st["avp"], st["ckp"], st["cvp"], st["bcp"], st["bsp"], st["fcp"],
            st["aks"], st["avs"], st["cks"], st["cvs"], st["bcs"], st["bss"], st["fcs"])
```
